```python
import math
import jax, jax.numpy as jnp
from jax import lax
import numpy as np

D_MODEL = 1024
BATCH = 4
SEQ = 8192
DEPTH = 2

N_EVEN = (DEPTH + 1) // 2
N_ODD = DEPTH // 2
EPS = 1e-6

SSD_D_INNER = D_MODEL
SSD_HEAD_DIM = 64
SSD_N_HEADS = SSD_D_INNER // SSD_HEAD_DIM
SSD_N_GROUPS = 2
SSD_D_STATE = 128
SSD_CONV = 4
SSD_CHUNK = 128
SSD_CONV_DIM = SSD_D_INNER + 2 * SSD_N_GROUPS * SSD_D_STATE

S5_WIDTH = D_MODEL
S5_GROUP = 16
S5_N_GROUPS = S5_WIDTH // S5_GROUP
S5_STATE = 64

HYB_IN = SSD_D_INNER + SSD_CONV_DIM + SSD_N_HEADS + S5_WIDTH
HYB_OUT = SSD_D_INNER + S5_WIDTH

ATT_HEAD_DIM = 128
ATT_KV_HEADS = D_MODEL // ATT_HEAD_DIM
ATT_PATTERNS = ((128, 1), (512, 4), (2048, 16))
ATT_N_PAT = len(ATT_PATTERNS)
ATT_Q_HEADS = ATT_N_PAT * ATT_KV_HEADS
ATT_BLOCK = 128
ATT_QKV = (ATT_Q_HEADS + 2 * ATT_KV_HEADS) * ATT_HEAD_DIM
ATT_OUT = ATT_KV_HEADS * ATT_HEAD_DIM

FFN_HIDDEN = -((-8 * D_MODEL) // (3 * 256)) * 256

kernel_name = "hybrid_ssd_s5_dilated_attn_trunk"


def rms_norm(x, g):
    x32 = x.astype(jnp.float32)
    y = x32 * lax.rsqrt(jnp.mean(x32 * x32, axis=-1, keepdims=True) + EPS)
    return (y * g.astype(jnp.float32)).astype(x.dtype)


def causal_dwconv(x, w, b):
    y = lax.conv_general_dilated(
        x, w[:, None, :].astype(x.dtype), window_strides=(1,),
        padding=[(w.shape[0] - 1, 0)], dimension_numbers=('NWC', 'WIO', 'NWC'),
        feature_group_count=x.shape[-1])
    return y + b.astype(x.dtype)


def ssd_scan(x, a, b, c):
    bs, l, h, p = x.shape
    g, n = b.shape[2], b.shape[3]
    r = h // g
    t = SSD_CHUNK
    nc = l // t
    xc = x.reshape(bs, nc, t, g, r, p)
    ac = jnp.cumsum(a.reshape(bs, nc, t, g, r), axis=2)
    bc = b.reshape(bs, nc, t, g, n)
    cc = c.reshape(bs, nc, t, g, n)
    causal = jnp.tril(jnp.ones((t, t), bool))[:, :, None, None]
    decay = jnp.exp(jnp.where(causal, ac[:, :, :, None] - ac[:, :, None, :], -jnp.inf))
    cb = jnp.einsum('bclgn,bcsgn->bclsg', cc, bc)
    y_diag = jnp.einsum('bclsg,bclsgr,bcsgrp->bclgrp', cb, decay, xc)
    decay_to_end = jnp.exp(ac[:, :, -1:] - ac)
    states = jnp.einsum('bcsgn,bcsgr,bcsgrp->bcgrpn', bc, decay_to_end, xc)
    chunk_decay = jnp.exp(ac[:, :, -1])

    def step(h_prev, inp):
        dec, st = inp
        return dec[..., None, None] * h_prev + st, h_prev

    h0 = jnp.zeros((bs, g, r, p, n), x.dtype)
    _, prev = lax.scan(step, h0, (jnp.moveaxis(chunk_decay, 1, 0), jnp.moveaxis(states, 1, 0)))
    prev = jnp.moveaxis(prev, 0, 1)
    y_off = jnp.einsum('bclgn,bcgrpn,bclgr->bclgrp', cc, prev, jnp.exp(ac))
    return (y_diag + y_off).reshape(bs, l, h, p)


def ssd_mixer(z, xbc, dt_raw, conv_w, conv_b, dt_bias, a_log, d_skip, norm_g):
    f32 = jnp.float32
    bs, l, _ = z.shape
    gn = SSD_N_GROUPS * SSD_D_STATE
    xbc = jax.nn.silu(causal_dwconv(xbc, conv_w, conv_b)).astype(f32)
    xs = xbc[..., :SSD_D_INNER].reshape(bs, l, SSD_N_HEADS, SSD_HEAD_DIM)
    bm = xbc[..., SSD_D_INNER:SSD_D_INNER + gn].reshape(bs, l, SSD_N_GROUPS, SSD_D_STATE)
    cm = xbc[..., SSD_D_INNER + gn:].reshape(bs, l, SSD_N_GROUPS, SSD_D_STATE)
    dt = jax.nn.softplus(dt_raw.astype(f32) + dt_bias.astype(f32))
    a = -jnp.exp(a_log.astype(f32))
    y = ssd_scan(xs * dt[..., None], dt * a, bm, cm) + d_skip.astype(f32)[:, None] * xs
    y = y.reshape(bs, l, SSD_D_INNER) * jax.nn.silu(z.astype(f32))
    yg = y.reshape(bs, l, SSD_N_GROUPS, SSD_D_INNER // SSD_N_GROUPS)
    yg = yg * lax.rsqrt(jnp.mean(yg * yg, axis=-1, keepdims=True) + EPS)
    return (yg.reshape(bs, l, SSD_D_INNER) * norm_g.astype(f32)).astype(z.dtype)


def complex_linear_combine(e1, e2):
    a1r, a1i, b1r, b1i = e1
    a2r, a2i, b2r, b2i = e2
    return (a2r * a1r - a2i * a1i, a2r * a1i + a2i * a1r,
            a2r * b1r - a2i * b1i + b2r, a2r * b1i + a2i * b1r + b2i)


def s5_mixer(u, lam_re, lam_im, log_dt, b_re, b_im, c_re, c_im, d_skip, glu_w, glu_b):
    f32 = jnp.float32
    bs, l, _ = u.shape
    lam_re, lam_im = lam_re.astype(f32), lam_im.astype(f32)
    dt = jnp.exp(log_dt.astype(f32))[:, None]
    mag = jnp.exp(lam_re * dt)
    a_re, a_im = mag * jnp.cos(lam_im * dt), mag * jnp.sin(lam_im * dt)
    den = lam_re * lam_re + lam_im * lam_im
    q_re = ((a_re - 1.0) * lam_re + a_im * lam_im) / den
    q_im = (a_im * lam_re - (a_re - 1.0) * lam_im) / den
    b_re, b_im = b_re.astype(f32), b_im.astype(f32)
    bb_re = q_re[..., None] * b_re - q_im[..., None] * b_im
    bb_im = q_re[..., None] * b_im + q_im[..., None] * b_re
    c_re, c_im = c_re.astype(f32), c_im.astype(f32)
    u_g = u.astype(f32).reshape(bs, l, S5_N_GROUPS, S5_GROUP)

    def run_sequence(us):
        bu_re = jnp.einsum('lgc,gpc->lgp', us, bb_re)
        bu_im = jnp.einsum('lgc,gpc->lgp', us, bb_im)
        ar = jnp.broadcast_to(a_re, bu_re.shape)
        ai = jnp.broadcast_to(a_im, bu_re.shape)
        _, _, h_re, h_im = lax.associative_scan(complex_linear_combine, (ar, ai, bu_re, bu_im), axis=0)
        return jnp.einsum('lgp,gcp->lgc', h_re, c_re) - jnp.einsum('lgp,gcp->lgc', h_im, c_im)

    y = lax.map(run_sequence, u_g)
    y = y + d_skip.astype(f32).reshape(S5_N_GROUPS, S5_GROUP) * u_g
    y = jax.nn.gelu(y.reshape(bs, l, S5_WIDTH), approximate=False)
    y = y * jax.nn.sigmoid(y @ glu_w.astype(f32) + glu_b.astype(f32))
    return y.astype(u.dtype)


def dilated_window_attention(q, k, v, window, dilation):
    bs, l, h, e = q.shape
    span = window // dilation
    unit = dilation * ATT_BLOCK
    lp = -(-l // unit) * unit
    m = lp // dilation
    nb = m // ATT_BLOCK

    def to_blocks(t):
        t = jnp.pad(t, ((0, 0), (0, lp - l), (0, 0), (0, 0))).reshape(bs, m, dilation, h, e)
        return jnp.swapaxes(t, 1, 2).reshape(bs, dilation, nb, ATT_BLOCK, h, e)

    def with_prev(t):
        prev = jnp.pad(t, ((0, 0), (0, 0), (1, 0), (0, 0), (0, 0), (0, 0)))[:, :, :-1]
        return jnp.concatenate([prev, t], axis=3)

    qb = to_blocks(q)
    kb = with_prev(to_blocks(k))
    vb = with_prev(to_blocks(v))
    s = jnp.einsum('brnqhe,brnkhe->brnhqk', qb, kb).astype(jnp.float32) * (e ** -0.5)
    qi = jnp.arange(ATT_BLOCK)[:, None]
    kj = jnp.arange(2 * ATT_BLOCK)[None, :]
    dist = ATT_BLOCK + qi - kj
    band = (dist >= 0) & (dist <= span)
    has_prev = jnp.arange(nb)[:, None, None] > 0
    valid = band[None] & (has_prev | (kj >= ATT_BLOCK)[None])
    s = jnp.where(valid[:, None], s, -jnp.inf)
    s_max = jnp.max(s, axis=-1, keepdims=True)
    p = jnp.exp(s - s_max)
    den = jnp.sum(p, axis=-1, keepdims=True)
    o = jnp.einsum('brnhqk,brnkhe->brnqhe', p / den, vb.astype(jnp.float32))
    lse = (s_max + jnp.log(den))[..., 0]
    o = jnp.swapaxes(o.reshape(bs, dilation, m, h, e), 1, 2).reshape(bs, lp, h, e)[:, :l]
    lse = jnp.swapaxes(jnp.swapaxes(lse, 3, 4).reshape(bs, dilation, m, h), 1, 2).reshape(bs, lp, h)[:, :l]
    return o, lse


def dilated_attention_mixer(h, w_qkv, w_o):
    bs, l, _ = h.shape
    nq = ATT_Q_HEADS * ATT_HEAD_DIM
    nk = ATT_KV_HEADS * ATT_HEAD_DIM
    qkv = h @ w_qkv
    q = qkv[..., :nq].reshape(bs, l, ATT_N_PAT, ATT_KV_HEADS, ATT_HEAD_DIM)
    k = qkv[..., nq:nq + nk].reshape(bs, l, ATT_KV_HEADS, ATT_HEAD_DIM)
    v = qkv[..., nq + nk:].reshape(bs, l, ATT_KV_HEADS, ATT_HEAD_DIM)
    outs, lses = [], []
    for i, (window, dilation) in enumerate(ATT_PATTERNS):
        o, lse = dilated_window_attention(q[:, :, i], k, v, window, dilation)
        outs.append(o)
        lses.append(lse)
    wts = jax.nn.softmax(jnp.stack(lses), axis=0)
    o = jnp.sum(wts[..., None] * jnp.stack(outs), axis=0)
    return o.reshape(bs, l, ATT_OUT).astype(h.dtype) @ w_o


def swiglu(h, w_in, w_out):
    g, u = jnp.split(h @ w_in, 2, axis=-1)
    return (jax.nn.silu(g) * u) @ w_out


def setup_inputs(seed: int = 0) -> dict:
    key = jax.random.key(seed)
    ks = jax.random.split(key, 32)
    f32 = jnp.float32

    def nrm(k, shape, scale):
        return jax.random.normal(k, shape, f32) * scale

    D = D_MODEL
    dt0 = jnp.exp(jax.random.uniform(ks[12], (N_EVEN, SSD_N_HEADS), f32, math.log(1e-3), math.log(1e-1)))
    lam_im = jnp.pi * jnp.arange(S5_STATE, dtype=f32)
    return {
        "x": nrm(ks[0], (BATCH, SEQ, D), 1.0),
        "c": nrm(ks[1], (BATCH, D), 1.0),
        "ada_w": nrm(ks[2], (DEPTH, D, 6 * D), 0.5 * D ** -0.5),
        "ada_b": nrm(ks[3], (DEPTH, 6 * D), 0.02),
        "mix_pre_g": 1.0 + nrm(ks[4], (DEPTH, D), 0.02),
        "mix_post_g": 1.0 + nrm(ks[5], (DEPTH, D), 0.02),
        "ffn_pre_g": 1.0 + nrm(ks[6], (DEPTH, D), 0.02),
        "ffn_post_g": 1.0 + nrm(ks[7], (DEPTH, D), 0.02),
        "ffn_w_in": nrm(ks[8], (DEPTH, D, 2 * FFN_HIDDEN), D ** -0.5),
        "ffn_w_out": nrm(ks[9], (DEPTH, FFN_HIDDEN, D), FFN_HIDDEN ** -0.5),
        "hyb_w_in": nrm(ks[10], (N_EVEN, D, HYB_IN), D ** -0.5),
        "ssd_conv_w": nrm(ks[11], (N_EVEN, SSD_CONV, SSD_CONV_DIM), SSD_CONV ** -0.5),
        "ssd_conv_b": nrm(ks[13], (N_EVEN, SSD_CONV_DIM), 0.02),
        "ssd_dt_bias": dt0 + jnp.log(-jnp.expm1(-dt0)),
        "ssd_a_log": jnp.log(jax.random.uniform(ks[14], (N_EVEN, SSD_N_HEADS), f32, 1.0, 16.0)),
        "ssd_d": 1.0 + nrm(ks[15], (N_EVEN, SSD_N_HEADS), 0.1),
        "ssd_norm_g": 1.0 + nrm(ks[16], (N_EVEN, SSD_D_INNER), 0.02),
        "s5_lambda_re": -0.5 + nrm(ks[17], (N_EVEN, S5_N_GROUPS, S5_STATE), 0.01),
        "s5_lambda_im": lam_im + nrm(ks[18], (N_EVEN, S5_N_GROUPS, S5_STATE), 0.01),
        "s5_log_dt": jax.random.uniform(ks[19], (N_EVEN, S5_N_GROUPS), f32, math.log(1e-3), math.log(1e-1)),
        "s5_b_re": nrm(ks[20], (N_EVEN, S5_N_GROUPS, S5_STATE, S5_GROUP), (2 * S5_GROUP) ** -0.5),
        "s5_b_im": nrm(ks[21], (N_EVEN, S5_N_GROUPS, S5_STATE, S5_GROUP), (2 * S5_GROUP) ** -0.5),
        "s5_c_re": nrm(ks[22], (N_EVEN, S5_N_GROUPS, S5_GROUP, S5_STATE), (2 * S5_STATE) ** -0.5),
        "s5_c_im": nrm(ks[23], (N_EVEN, S5_N_GROUPS, S5_GROUP, S5_STATE), (2 * S5_STATE) ** -0.5),
        "s5_d": nrm(ks[24], (N_EVEN, S5_WIDTH), 1.0),
        "s5_glu_w": nrm(ks[25], (N_EVEN, S5_WIDTH, S5_WIDTH), S5_WIDTH ** -0.5),
        "s5_glu_b": nrm(ks[26], (N_EVEN, S5_WIDTH), 0.02),
        "hyb_w_out": nrm(ks[27], (N_EVEN, HYB_OUT, D), HYB_OUT ** -0.5),
        "attn_w_qkv": nrm(ks[28], (N_ODD, D, ATT_QKV), D ** -0.5),
        "attn_w_o": nrm(ks[29], (N_ODD, ATT_OUT, D), ATT_OUT ** -0.5),
    }


def reference(x, c, ada_w, ada_b, mix_pre_g, mix_post_g, ffn_pre_g, ffn_post_g,
              ffn_w_in, ffn_w_out, hyb_w_in, ssd_conv_w, ssd_conv_b, ssd_dt_bias,
              ssd_a_log, ssd_d, ssd_norm_g, s5_lambda_re, s5_lambda_im, s5_log_dt,
              s5_b_re, s5_b_im, s5_c_re, s5_c_im, s5_d, s5_glu_w, s5_glu_b,
              hyb_w_out, attn_w_qkv, attn_w_o):
    cond = jax.nn.silu(c)
    split_at = [SSD_D_INNER, SSD_D_INNER + SSD_CONV_DIM, SSD_D_INNER + SSD_CONV_DIM + SSD_N_HEADS]
    for i in range(DEPTH):
        mod = (cond @ ada_w[i] + ada_b[i])[:, None, :]
        sh_m, sc_m, gt_m, sh_f, sc_f, gt_f = jnp.split(mod, 6, axis=-1)
        h = rms_norm(x, mix_pre_g[i]) * (1 + sc_m) + sh_m
        j = i // 2
        if i % 2 == 0:
            proj = h @ hyb_w_in[j]
            z, xbc, dt_raw, u = jnp.split(proj, split_at, axis=-1)
            y_ssd = ssd_mixer(z, xbc, dt_raw, ssd_conv_w[j], ssd_conv_b[j], ssd_dt_bias[j],
                              ssd_a_log[j], ssd_d[j], ssd_norm_g[j])
            y_s5 = s5_mixer(u, s5_lambda_re[j], s5_lambda_im[j], s5_log_dt[j], s5_b_re[j],
                            s5_b_im[j], s5_c_re[j], s5_c_im[j], s5_d[j], s5_glu_w[j], s5_glu_b[j])
            y = jnp.concatenate([y_ssd, y_s5], axis=-1) @ hyb_w_out[j]
        else:
            y = dilated_attention_mixer(h, attn_w_qkv[j], attn_w_o[j])
        x = x + gt_m * rms_norm(y, mix_post_g[i])
        h = rms_norm(x, ffn_pre_g[i]) * (1 + sc_f) + sh_f
        x = x + gt_f * rms_norm(swiglu(h, ffn_w_in[i], ffn_w_out[i]), ffn_post_g[i])
    return x
```

```python
import functools
import math

import jax
import jax.numpy as jnp
from jax import lax
from jax.experimental import pallas as pl
from jax.experimental.pallas import tpu as pltpu

F32 = jnp.float32
BF16 = jnp.bfloat16
EPS = 1e-6
HIGHEST = lax.Precision.HIGHEST

LANES = 128
SUBLANES = 8
VMEM_LIMIT = 56 * 1024 * 1024

SSD_HEAD_DIM = 64
SSD_N_HEADS = 16
SSD_N_GROUPS = 2
SSD_D_STATE = 128
SSD_CONV = 4
SSD_CHUNK = 128
S5_GROUP = 16
S5_STATE = 64
ATT_HEAD_DIM = 128
ATT_PATTERNS = ((128, 1), (512, 4), (2048, 16))
ATT_BLOCK = 128
ATT_UNIT = ATT_BLOCK * 16


def _cparams(*sem):
    return pltpu.CompilerParams(dimension_semantics=sem, vmem_limit_bytes=VMEM_LIMIT)


def _resident(shape):
    nd = len(shape)
    return pl.BlockSpec(shape, lambda *_: (0,) * nd, pipeline_mode=pl.Buffered(1))


def _silu(v):
    return v * jax.nn.sigmoid(v)


def _normmod(x, g, sc, sh):
    ms = jnp.mean(x * x, axis=-1, keepdims=True)
    return (x * lax.rsqrt(ms + EPS) * g) * (1.0 + sc) + sh


def _post_residual(x, y, g, gt):
    ms = jnp.mean(y * y, axis=-1, keepdims=True)
    return x + gt * (y * lax.rsqrt(ms + EPS) * g)


def _ada_kernel(c_ref, w_ref, b_ref, o_ref):
    cond = _silu(c_ref[...])
    o_ref[...] = jnp.dot(cond, w_ref[...], precision=HIGHEST,
                         preferred_element_type=F32) + b_ref[...]


def _ada(c, ada_w, ada_b):
    depth, d, n = ada_w.shape
    b = c.shape[0]
    rows = -(-b // SUBLANES) * SUBLANES
    c_pad = jnp.zeros((rows, d), F32).at[:b].set(c)
    tn = 1536
    out = pl.pallas_call(
        _ada_kernel,
        grid=(depth, n // tn),
        in_specs=[pl.BlockSpec((rows, d), lambda i, j: (0, 0)),
                  pl.BlockSpec((None, d, tn), lambda i, j: (i, 0, j)),
                  pl.BlockSpec((None, 1, tn), lambda i, j: (i, 0, j))],
        out_specs=pl.BlockSpec((None, rows, tn), lambda i, j: (i, 0, j)),
        out_shape=jax.ShapeDtypeStruct((depth, rows, n), F32),
        compiler_params=_cparams("arbitrary", "arbitrary"),
    )(c_pad, ada_w, ada_b.reshape(depth, 1, n))
    return out[:, :b]


def _inproj_kernel(x_ref, g_ref, sc_ref, sh_ref, w_ref, *o_refs, splits, slabbed):
    h = _normmod(x_ref[...], g_ref[...], sc_ref[...], sh_ref[...]).astype(BF16)
    off = 0
    for o_ref, n, slab in zip(o_refs, splits, slabbed):
        r = jnp.dot(h, w_ref[:, off:off + n], preferred_element_type=F32)
        if slab:
            for k in range(n // LANES):
                o_ref[k] = r[:, k * LANES:(k + 1) * LANES]
        else:
            o_ref[...] = r
        off += n


def _inproj(x, g, sc, sh, w, splits, slabbed, tm=512):
    b, l, d = x.shape
    row = lambda bi, i: (bi, i, 0)
    per_b = lambda bi, i: (bi, 0, 0)
    slab_row = lambda bi, i: (bi, 0, i, 0)
    return pl.pallas_call(
        functools.partial(_inproj_kernel, splits=splits, slabbed=slabbed),
        grid=(b, l // tm),
        in_specs=[pl.BlockSpec((None, tm, d), row),
                  _resident((1, d)),
                  pl.BlockSpec((None, 1, d), per_b),
                  pl.BlockSpec((None, 1, d), per_b),
                  _resident(w.shape)],
        out_specs=[pl.BlockSpec((None, n // LANES, tm, LANES), slab_row) if s
                   else pl.BlockSpec((None, tm, n), row) for n, s in zip(splits, slabbed)],
        out_shape=[jax.ShapeDtypeStruct((b, n // LANES, l, LANES) if s else (b, l, n), F32)
                   for n, s in zip(splits, slabbed)],
        compiler_params=_cparams("parallel", "parallel"),
    )(x, g, sc, sh, w)


def _qkvproj_kernel(x_ref, g_ref, sc_ref, sh_ref, w_ref, o_ref, *, n_heads):
    h = _normmod(x_ref[...], g_ref[...], sc_ref[...], sh_ref[...]).astype(BF16)
    group = 8
    for h0 in range(0, n_heads, group):
        r = jnp.dot(h, w_ref[:, h0 * ATT_HEAD_DIM:(h0 + group) * ATT_HEAD_DIM],
                    preferred_element_type=F32)
        for k in range(group):
            o_ref[h0 + k] = r[:, k * ATT_HEAD_DIM:(k + 1) * ATT_HEAD_DIM]


def _qkvproj(x, g, sc, sh, w, tm=512):
    b, l, d = x.shape
    n_heads = w.shape[1] // ATT_HEAD_DIM
    per_b = lambda bi, i: (bi, 0, 0)
    return pl.pallas_call(
        functools.partial(_qkvproj_kernel, n_heads=n_heads),
        grid=(b, l // tm),
        in_specs=[pl.BlockSpec((None, tm, d), lambda bi, i: (bi, i, 0)),
                  _resident((1, d)),
                  pl.BlockSpec((None, 1, d), per_b),
                  pl.BlockSpec((None, 1, d), per_b),
                  _resident(w.shape)],
        out_specs=pl.BlockSpec((None, n_heads, tm, ATT_HEAD_DIM), lambda bi, i: (bi, 0, i, 0)),
        out_shape=jax.ShapeDtypeStruct((b, n_heads, l, ATT_HEAD_DIM), F32),
        compiler_params=_cparams("parallel", "parallel"),
    )(x, g, sc, sh, w)


def _outproj_kernel(*refs, n_in):
    acts = refs[:n_in]
    ws = refs[n_in:2 * n_in]
    x_ref, g_ref, gt_ref, o_ref = refs[2 * n_in:]
    y = None
    for a_ref, w_ref in zip(acts, ws):
        if len(a_ref.shape) == 3:
            a = jnp.concatenate([a_ref[k] for k in range(a_ref.shape[0])], axis=1)
        else:
            a = a_ref[...]
        t = jnp.dot(a.astype(BF16), w_ref[...], preferred_element_type=F32)
        y = t if y is None else y + t
    o_ref[...] = _post_residual(x_ref[...], y, g_ref[...], gt_ref[...])


def _outproj(acts, ws, x, g, gt, tm=512):
    b, l, d = x.shape
    row = lambda bi, i: (bi, i, 0)
    n_in = len(acts)
    act_spec = lambda a: (pl.BlockSpec((None, tm, a.shape[-1]), row) if a.ndim == 3 else
                          pl.BlockSpec((None, a.shape[1], tm, LANES), lambda bi, i: (bi, 0, i, 0)))
    return pl.pallas_call(
        functools.partial(_outproj_kernel, n_in=n_in),
        grid=(b, l // tm),
        in_specs=([act_spec(a) for a in acts]
                  + [_resident(w.shape) for w in ws]
                  + [pl.BlockSpec((None, tm, d), row),
                     _resident((1, d)),
                     pl.BlockSpec((None, 1, d), lambda bi, i: (bi, 0, 0))]),
        out_specs=pl.BlockSpec((None, tm, d), row),
        out_shape=jax.ShapeDtypeStruct((b, l, d), F32),
        compiler_params=_cparams("parallel", "parallel"),
    )(*acts, *ws, x, g, gt)


def _ffn_kernel(x_ref, gpre_ref, sc_ref, sh_ref, win_ref, wout_ref, gpost_ref, gt_ref,
                o_ref, act_ref, *, hidden, chunk):
    x = x_ref[...]
    h = _normmod(x, gpre_ref[...], sc_ref[...], sh_ref[...]).astype(BF16)
    for c0 in range(0, hidden, chunk):
        gate = jnp.dot(h, win_ref[:, c0:c0 + chunk], preferred_element_type=F32)
        up = jnp.dot(h, win_ref[:, hidden + c0:hidden + c0 + chunk], preferred_element_type=F32)
        act_ref[:, c0:c0 + chunk] = (_silu(gate) * up).astype(BF16)
    y = jnp.dot(act_ref[...], wout_ref[...], preferred_element_type=F32)
    o_ref[...] = _post_residual(x, y, gpost_ref[...], gt_ref[...])


def _ffn(x, gpre, sc, sh, win, wout, gpost, gt, tm=512, chunk=256):
    b, l, d = x.shape
    hidden = wout.shape[0]
    row = lambda bi, i: (bi, i, 0)
    per_b = lambda bi, i: (bi, 0, 0)
    return pl.pallas_call(
        functools.partial(_ffn_kernel, hidden=hidden, chunk=chunk),
        grid=(b, l // tm),
        in_specs=[pl.BlockSpec((None, tm, d), row),
                  _resident((1, d)),
                  pl.BlockSpec((None, 1, d), per_b),
                  pl.BlockSpec((None, 1, d), per_b),
                  _resident(win.shape),
                  _resident(wout.shape),
                  _resident((1, d)),
                  pl.BlockSpec((None, 1, d), per_b)],
        out_specs=pl.BlockSpec((None, tm, d), row),
        out_shape=jax.ShapeDtypeStruct((b, l, d), F32),
        scratch_shapes=[pltpu.VMEM((tm, hidden), BF16)],
        compiler_params=_cparams("parallel", "parallel"),
    )(x, gpre, sc, sh, win, wout, gpost, gt)


def _split2(v):
    hi = v.astype(BF16)
    return hi, (v - hi.astype(F32)).astype(BF16)


def _split3(v):
    hi = v.astype(BF16)
    r = v - hi.astype(F32)
    mid = r.astype(BF16)
    return hi, mid, (r - mid.astype(F32)).astype(BF16)


def _expand(parts, e):
    out = None
    for p in parts:
        t = jnp.dot(p, e, preferred_element_type=F32)
        out = t if out is None else out + t
    return out


def _ssd_kernel(z_ref, xbc_ref, dt_ref, cw_ref, cb_ref, dtb_ref, alog_ref, dsk_ref, ng_ref,
                e64_ref, e128_ref, o_ref, ext_ref, h_ref, *, ts):
    t = SSD_CHUNK
    d_inner = SSD_N_HEADS * SSD_HEAD_DIM
    gw = d_inner // SSD_N_GROUPS
    hpg = SSD_N_HEADS // SSD_N_GROUPS
    pad = SUBLANES

    @pl.when(pl.program_id(1) == 0)
    def _():
        ext_ref[0:pad, :] = jnp.zeros((pad, ext_ref.shape[1]), F32)
        h_ref[...] = jnp.zeros(h_ref.shape, F32)

    ext_ref[pad:pad + ts, :] = xbc_ref[...]

    li = lax.broadcasted_iota(jnp.int32, (t, t), 0)
    si = lax.broadcasted_iota(jnp.int32, (t, t), 1)
    causal = li >= si
    tril = causal.astype(F32)
    lane = lax.broadcasted_iota(jnp.int32, (t, LANES), 1)
    low_half = lane < SSD_HEAD_DIM
    a_neg = -jnp.exp(alog_ref[...])
    e64 = e64_ref[...]
    e128 = e128_ref[...]

    for c in range(ts // t):
        r0 = c * t
        acc = cb_ref[...] + cw_ref[0:1, :] * ext_ref[pad + r0 - 3:pad + r0 - 3 + t, :]
        for k in range(1, SSD_CONV):
            acc = acc + cw_ref[k:k + 1, :] * ext_ref[pad + r0 - 3 + k:pad + r0 - 3 + k + t, :]
        xc = _silu(acc)
        xs = xc[:, :d_inner]
        dt = jax.nn.softplus(dt_ref[r0:r0 + t, :] + dtb_ref[...])
        ac = jnp.dot(tril, dt * a_neg, precision=HIGHEST, preferred_element_type=F32)
        ac_t = ac.T
        ac_last = ac[t - 1:t, :]
        dt_e = _expand(_split2(dt), e64)
        sdec_e = _expand(_split2(dt * jnp.exp(ac_last - ac)), e64)
        eac_e = _expand(_split2(jnp.exp(ac)), e64)
        cdec_e = _expand(_split2(jnp.broadcast_to(jnp.exp(ac_last), (SUBLANES, LANES))), e64)[0:1, :]
        ac_e = _expand(_split3(ac), e128)
        x_dt = (xs * dt_e).astype(BF16)
        x_st = (xs * sdec_e).astype(BF16)
        ys = []
        for g in range(SSD_N_GROUPS):
            bm = xc[:, d_inner + g * SSD_D_STATE:d_inner + (g + 1) * SSD_D_STATE]
            cm = xc[:, d_inner + (SSD_N_GROUPS + g) * SSD_D_STATE:
                    d_inner + (SSD_N_GROUPS + g + 1) * SSD_D_STATE].astype(BF16)
            cb = lax.dot_general(cm, bm.astype(BF16), (((1,), (1,)), ((), ())),
                                 preferred_element_type=F32)
            gs = slice(g * gw, (g + 1) * gw)
            h_prev = h_ref[:, gs]
            y_off = jnp.dot(cm, h_prev.astype(BF16), preferred_element_type=F32) * eac_e[:, gs]
            s_new = jnp.dot(bm.T.astype(BF16), x_st[:, gs], preferred_element_type=F32)
            h_ref[:, gs] = h_prev * cdec_e[:, gs] + s_new
            pairs = []
            for q in range(hpg // 2):
                ps = []
                for k in range(2):
                    hh = g * hpg + 2 * q + k
                    diff = ac_e[:, hh * LANES:(hh + 1) * LANES] - ac_t[hh:hh + 1, :]
                    decay = jnp.where(causal, jnp.exp(jnp.minimum(diff, 0.0)), 0.0)
                    ps.append((cb * decay).astype(BF16))
                xp = x_dt[:, g * gw + q * LANES:g * gw + (q + 1) * LANES]
                zero = jnp.zeros_like(xp)
                lhs = jnp.concatenate(ps, axis=1)
                rhs = jnp.concatenate([jnp.where(low_half, xp, zero),
                                       jnp.where(low_half, zero, xp)], axis=0)
                pairs.append(jnp.dot(lhs, rhs, preferred_element_type=F32))
            ys.append(jnp.concatenate(pairs, axis=1) + y_off)
        y = jnp.concatenate(ys, axis=1) + dsk_ref[...] * xs
        y = y * _silu(z_ref[r0:r0 + t, :])
        outs = []
        for g in range(SSD_N_GROUPS):
            yg = y[:, g * gw:(g + 1) * gw]
            outs.append(yg * lax.rsqrt(jnp.mean(yg * yg, axis=-1, keepdims=True) + EPS))
        o_ref[r0:r0 + t, :] = jnp.concatenate(outs, axis=1) * ng_ref[...]

    ext_ref[0:pad, :] = ext_ref[ts:ts + pad, :]


def _ssd(z, xbc, dt, conv_w, conv_b, dt_bias, a_log, d_skip, norm_g, ts=256):
    b, l, d_inner = z.shape
    cdim = xbc.shape[-1]
    nh = SSD_N_HEADS
    pad1 = lambda v: jnp.zeros((1, LANES), F32).at[0, :nh].set(v)
    head = jnp.arange(LANES)[:, None]
    e64 = (head == (jnp.arange(d_inner) // SSD_HEAD_DIM)[None, :]) & (head < nh)
    e128 = (head == (jnp.arange(nh * LANES) // LANES)[None, :]) & (head < nh)
    row = lambda bi, i: (bi, i, 0)
    return pl.pallas_call(
        functools.partial(_ssd_kernel, ts=ts),
        grid=(b, l // ts),
        in_specs=[pl.BlockSpec((None, ts, d_inner), row),
                  pl.BlockSpec((None, ts, cdim), row),
                  pl.BlockSpec((None, ts, LANES), row),
                  _resident(conv_w.shape),
                  _resident((1, cdim)),
                  _resident((1, LANES)),
                  _resident((1, LANES)),
                  _resident((1, d_inner)),
                  _resident((1, d_inner)),
                  _resident((LANES, d_inner)),
                  _resident((LANES, nh * LANES))],
        out_specs=pl.BlockSpec((None, ts, d_inner), row),
        out_shape=jax.ShapeDtypeStruct((b, l, d_inner), F32),
        scratch_shapes=[pltpu.VMEM((SUBLANES + ts, cdim), F32),
                        pltpu.VMEM((SSD_D_STATE, d_inner), F32)],
        compiler_params=_cparams("parallel", "arbitrary"),
    )(z, xbc, dt, conv_w, conv_b.reshape(1, cdim), pad1(dt_bias), pad1(a_log),
      jnp.repeat(d_skip, SSD_HEAD_DIM).reshape(1, d_inner), norm_g.reshape(1, d_inner),
      e64.astype(BF16), e128.astype(BF16))


def _s5_kernel(u_ref, bmat_ref, cmat_ref, are_ref, aim_ref, asre_ref, asim_ref, dsk_ref,
               gluw_ref, glub_ref, o_ref, up_ref, hre_ref, him_ref, ere_ref, eim_ref,
               cre_ref, cim_ref, y_ref, *, tile, unroll):
    seg = tile // SUBLANES
    n_slab = u_ref.shape[0]
    sw = hre_ref.shape[-1] // n_slab

    @pl.when(pl.program_id(1) == 0)
    def _():
        cre_ref[...] = jnp.zeros(cre_ref.shape, F32)
        cim_ref[...] = jnp.zeros(cim_ref.shape, F32)

    def permute_in(i, carry):
        rows = pl.ds(pl.multiple_of(i * SUBLANES, SUBLANES), SUBLANES)
        for j in range(n_slab):
            up_ref[rows, j * LANES:(j + 1) * LANES] = u_ref[j, pl.ds(i, SUBLANES, stride=seg), :]
        return carry
    lax.fori_loop(0, seg, permute_in, 0)

    for j in range(n_slab):
        bu = jnp.dot(up_ref[:, j * LANES:(j + 1) * LANES].astype(BF16), bmat_ref[j],
                     preferred_element_type=F32)
        hre_ref[:, j * sw:(j + 1) * sw] = bu[:, :sw]
        him_ref[:, j * sw:(j + 1) * sw] = bu[:, sw:]

    def scan(j, init_re, init_im, store):
        cols = slice(j * sw, (j + 1) * sw)
        a_re = jnp.broadcast_to(are_ref[:, cols], (SUBLANES, sw))
        a_im = jnp.broadcast_to(aim_ref[:, cols], (SUBLANES, sw))

        def body(i, carry):
            s_re, s_im = carry
            rows = pl.ds(pl.multiple_of(i * SUBLANES, SUBLANES), SUBLANES)
            n_re = a_re * s_re - a_im * s_im + hre_ref[rows, cols]
            n_im = a_re * s_im + a_im * s_re + him_ref[rows, cols]
            if store:
                hre_ref[rows, cols] = n_re
                him_ref[rows, cols] = n_im
            return n_re, n_im
        return lax.fori_loop(0, seg, body, (init_re, init_im), unroll=unroll)

    zeros = jnp.zeros((SUBLANES, sw), F32)
    for j in range(n_slab):
        e_re, e_im = scan(j, zeros, zeros, store=False)
        ere_ref[:, j * sw:(j + 1) * sw] = e_re
        eim_ref[:, j * sw:(j + 1) * sw] = e_im

    s_re, s_im = cre_ref[...], cim_ref[...]
    as_re, as_im = asre_ref[...], asim_ref[...]
    in_re, in_im = [], []
    for k in range(SUBLANES):
        in_re.append(s_re)
        in_im.append(s_im)
        e_re, e_im = ere_ref[k:k + 1, :], eim_ref[k:k + 1, :]
        s_re, s_im = (as_re * s_re - as_im * s_im + e_re, as_re * s_im + as_im * s_re + e_im)
    cre_ref[...] = s_re
    cim_ref[...] = s_im
    ere_ref[...] = jnp.concatenate(in_re, axis=0)
    eim_ref[...] = jnp.concatenate(in_im, axis=0)

    for j in range(n_slab):
        cols = slice(j * sw, (j + 1) * sw)
        scan(j, ere_ref[:, cols], eim_ref[:, cols], store=True)

    for j in range(n_slab):
        cols = slice(j * sw, (j + 1) * sw)
        ch = slice(j * LANES, (j + 1) * LANES)
        y = (jnp.dot(hre_ref[:, cols].astype(BF16), cmat_ref[j, :sw, :], preferred_element_type=F32)
             + jnp.dot(him_ref[:, cols].astype(BF16), cmat_ref[j, sw:, :], preferred_element_type=F32))
        y = y + dsk_ref[:, ch] * up_ref[:, ch]
        y_ref[:, ch] = 0.5 * y * (1.0 + lax.erf(y * (2.0 ** -0.5)))

    y = y_ref[...]
    gate = jnp.dot(y.astype(BF16), gluw_ref[...], preferred_element_type=F32) + glub_ref[...]
    y_ref[...] = y * jax.nn.sigmoid(gate)

    def permute_out(i, carry):
        rows = pl.ds(pl.multiple_of(i * SUBLANES, SUBLANES), SUBLANES)
        for j in range(n_slab):
            o_ref[j, pl.ds(i, SUBLANES, stride=seg), :] = y_ref[rows, j * LANES:(j + 1) * LANES]
        return carry
    lax.fori_loop(0, seg, permute_out, 0)


def _s5_params(lam_re, lam_im, log_dt, b_re, b_im, c_re, c_im, seg):
    ng, ns = lam_re.shape
    gps = LANES // S5_GROUP
    n_slab = ng // gps
    dt = jnp.exp(log_dt)[:, None]
    mag = jnp.exp(lam_re * dt)
    a_re, a_im = mag * jnp.cos(lam_im * dt), mag * jnp.sin(lam_im * dt)
    den = lam_re * lam_re + lam_im * lam_im
    q_re = ((a_re - 1.0) * lam_re + a_im * lam_im) / den
    q_im = (a_im * lam_re - (a_re - 1.0) * lam_im) / den
    bb_re = q_re[..., None] * b_re - q_im[..., None] * b_im
    bb_im = q_re[..., None] * b_im + q_im[..., None] * b_re
    eye = jnp.eye(gps, dtype=F32)

    def in_map(bb):
        t = bb.reshape(n_slab, gps, ns, S5_GROUP)
        return jnp.einsum('jgpc,gh->jgchp', t, eye).reshape(n_slab, LANES, gps * ns)

    def out_map(cc):
        t = cc.reshape(n_slab, gps, S5_GROUP, ns)
        return jnp.einsum('jgcp,gh->jgphc', t, eye).reshape(n_slab, gps * ns, LANES)

    bmat = jnp.concatenate([in_map(bb_re), in_map(bb_im)], axis=2).astype(BF16)
    cmat = jnp.concatenate([out_map(c_re), -out_map(c_im)], axis=1).astype(BF16)
    as_re, as_im = a_re, a_im
    for _ in range(int(math.log2(seg))):
        as_re, as_im = as_re * as_re - as_im * as_im, 2.0 * as_re * as_im
    flat = lambda v: v.reshape(1, ng * ns)
    return bmat, cmat, flat(a_re), flat(a_im), flat(as_re), flat(as_im)


def _s5(u, lam_re, lam_im, log_dt, b_re, b_im, c_re, c_im, d_skip, glu_w, glu_b, tile=256):
    b, n_slab, l, _ = u.shape
    width = n_slab * LANES
    seg = tile // SUBLANES
    assert seg & (seg - 1) == 0
    bmat, cmat, a_re, a_im, as_re, as_im = _s5_params(lam_re, lam_im, log_dt, b_re, b_im,
                                                      c_re, c_im, seg)
    n_state = a_re.shape[-1]
    row = lambda bi, i: (bi, 0, i, 0)
    vec = _resident((1, n_state))
    return pl.pallas_call(
        functools.partial(_s5_kernel, tile=tile, unroll=4),
        grid=(b, l // tile),
        in_specs=[pl.BlockSpec((None, n_slab, tile, LANES), row),
                  _resident(bmat.shape), _resident(cmat.shape),
                  vec, vec, vec, vec,
                  _resident((1, width)),
                  _resident(glu_w.shape),
                  _resident((1, width))],
        out_specs=pl.BlockSpec((None, n_slab, tile, LANES), row),
        out_shape=jax.ShapeDtypeStruct((b, n_slab, l, LANES), F32),
        scratch_shapes=[pltpu.VMEM((tile, width), F32),
                        pltpu.VMEM((tile, n_state), F32),
                        pltpu.VMEM((tile, n_state), F32),
                        pltpu.VMEM((SUBLANES, n_state), F32),
                        pltpu.VMEM((SUBLANES, n_state), F32),
                        pltpu.VMEM((1, n_state), F32),
                        pltpu.VMEM((1, n_state), F32),
                        pltpu.VMEM((tile, width), F32)],
        compiler_params=_cparams("parallel", "arbitrary"),
    )(u, bmat, cmat, a_re, a_im, as_re, as_im, d_skip.reshape(1, width),
      glu_w.astype(BF16), glu_b.reshape(1, width))


def _attn_kernel(q0_ref, q1_ref, q2_ref, kc_ref, vc_ref, kp_ref, vp_ref, o_ref,
                 acc_ref, lse_ref):
    blk = ATT_BLOCK
    scale = ATT_HEAD_DIM ** -0.5
    qi = lax.broadcasted_iota(jnp.int32, (blk, blk), 0)
    kj = lax.broadcasted_iota(jnp.int32, (blk, blk), 1)
    cur_ok = kj <= qi
    prev_ok = kj >= qi
    first_ok = kj >= qi + jnp.where(pl.program_id(1) > 0, 0, blk)
    neg = jnp.float32(-jnp.inf)
    dn = (((1,), (1,)), ((), ()))

    for p, (q_ref, (window, dil)) in enumerate(zip((q0_ref, q1_ref, q2_ref), ATT_PATTERNS)):
        assert window // dil == blk
        n_blk = ATT_UNIT // (blk * dil)
        for r in range(dil):
            for bi in range(n_blk):
                start = r + blk * dil * bi
                rows = pl.ds(start, blk, stride=dil) if dil > 1 else pl.ds(start, blk)
                if bi > 0:
                    pstart = start - blk * dil
                    k_src, v_src, p_valid = kc_ref, vc_ref, prev_ok
                else:
                    pstart = r + blk * dil * (n_blk - 1)
                    k_src, v_src, p_valid = kp_ref, vp_ref, first_ok
                prows = pl.ds(pstart, blk, stride=dil) if dil > 1 else pl.ds(pstart, blk)
                q = (q_ref[rows, :] * scale).astype(BF16)
                s_c = lax.dot_general(q, kc_ref[rows, :].astype(BF16), dn, preferred_element_type=F32)
                s_p = lax.dot_general(q, k_src[prows, :].astype(BF16), dn, preferred_element_type=F32)
                s_c = jnp.where(cur_ok, s_c, neg)
                s_p = jnp.where(p_valid, s_p, neg)
                m = jnp.max(jnp.maximum(s_c, s_p), axis=-1, keepdims=True)
                e_c = jnp.exp(s_c - m)
                e_p = jnp.exp(s_p - m)
                den = jnp.sum(e_c + e_p, axis=-1, keepdims=True)
                o = (jnp.dot(e_c.astype(BF16), vc_ref[rows, :].astype(BF16), preferred_element_type=F32)
                     + jnp.dot(e_p.astype(BF16), v_src[prows, :].astype(BF16), preferred_element_type=F32))
                acc_ref[p, rows, :] = o / den
                lse_ref[p, rows, :] = jnp.broadcast_to(m + jnp.log(den), (blk, ATT_HEAD_DIM))

    l0, l1, l2 = lse_ref[0], lse_ref[1], lse_ref[2]
    mx = jnp.maximum(jnp.maximum(l0, l1), l2)
    w0, w1, w2 = jnp.exp(l0 - mx), jnp.exp(l1 - mx), jnp.exp(l2 - mx)
    o_ref[...] = (w0 * acc_ref[0] + w1 * acc_ref[1] + w2 * acc_ref[2]) / (w0 + w1 + w2)


def _attention(qkv, n_kv):
    b, _, l, e = qkv.shape
    n_pat = len(ATT_PATTERNS)
    unit = ATT_UNIT
    tile = (None, None, unit, e)
    q_spec = lambda p: pl.BlockSpec(tile, lambda bi, u, h, p=p: (bi, p * n_kv + h, u, 0))
    kv_cur = lambda off: pl.BlockSpec(tile, lambda bi, u, h, off=off: (bi, off + h, u, 0))
    kv_prev = lambda off: pl.BlockSpec(
        tile, lambda bi, u, h, off=off: (bi, off + h, jnp.maximum(u - 1, 0), 0))
    k_off, v_off = n_pat * n_kv, (n_pat + 1) * n_kv
    return pl.pallas_call(
        _attn_kernel,
        grid=(b, l // unit, n_kv),
        in_specs=[q_spec(0), q_spec(1), q_spec(2), kv_cur(k_off), kv_cur(v_off),
                  kv_prev(k_off), kv_prev(v_off)],
        out_specs=pl.BlockSpec((None, unit, e), lambda bi, u, h: (bi, u, h)),
        out_shape=jax.ShapeDtypeStruct((b, l, n_kv * e), F32),
        scratch_shapes=[pltpu.VMEM((n_pat, unit, e), F32),
                        pltpu.VMEM((n_pat, unit, e), F32)],
        compiler_params=_cparams("parallel", "parallel", "parallel"),
    )(qkv, qkv, qkv, qkv, qkv, qkv, qkv)


def kernel(x, c, ada_w, ada_b, mix_pre_g, mix_post_g, ffn_pre_g, ffn_post_g, ffn_w_in, ffn_w_out, hyb_w_in, ssd_conv_w, ssd_conv_b, ssd_dt_bias, ssd_a_log, ssd_d, ssd_norm_g, s5_lambda_re, s5_lambda_im, s5_log_dt, s5_b_re, s5_b_im, s5_c_re, s5_c_im, s5_d, s5_glu_w, s5_glu_b, hyb_w_out, attn_w_qkv, attn_w_o):
    bsz, seq, d = x.shape
    depth = ada_w.shape[0]
    assert seq % ATT_UNIT == 0 and d == SSD_N_HEADS * SSD_HEAD_DIM
    d_inner = d
    conv_dim = ssd_conv_w.shape[-1]
    n_kv = d // ATT_HEAD_DIM

    mod = _ada(c, ada_w, ada_b)
    vec = lambda v: v.reshape(1, d)

    for i in range(depth):
        sh_m, sc_m, gt_m, sh_f, sc_f, gt_f = (
            mod[i, :, k * d:(k + 1) * d].reshape(bsz, 1, d) for k in range(6))
        j = i // 2
        if i % 2 == 0:
            w = hyb_w_in[j]
            o_dt = d_inner + conv_dim
            w_dt = jnp.zeros((d, LANES), F32).at[:, :SSD_N_HEADS].set(w[:, o_dt:o_dt + SSD_N_HEADS])
            w_cat = jnp.concatenate([w[:, :o_dt], w[:, o_dt + SSD_N_HEADS:], w_dt], axis=1).astype(BF16)
            z, xbc, u, dt = _inproj(x, vec(mix_pre_g[i]), sc_m, sh_m, w_cat,
                                    (d_inner, conv_dim, d, LANES), (False, False, True, False))
            y_ssd = _ssd(z, xbc, dt, ssd_conv_w[j], ssd_conv_b[j], ssd_dt_bias[j], ssd_a_log[j],
                         ssd_d[j], ssd_norm_g[j])
            y_s5 = _s5(u, s5_lambda_re[j], s5_lambda_im[j], s5_log_dt[j], s5_b_re[j], s5_b_im[j],
                       s5_c_re[j], s5_c_im[j], s5_d[j], s5_glu_w[j], s5_glu_b[j])
            w_out = hyb_w_out[j].astype(BF16)
            x = _outproj([y_ssd, y_s5], [w_out[:d_inner], w_out[d_inner:]], x,
                         vec(mix_post_g[i]), gt_m)
        else:
            qkv = _qkvproj(x, vec(mix_pre_g[i]), sc_m, sh_m, attn_w_qkv[j].astype(BF16))
            o = _attention(qkv, n_kv)
            x = _outproj([o], [attn_w_o[j].astype(BF16)], x, vec(mix_post_g[i]), gt_m)
        x = _ffn(x, vec(ffn_pre_g[i]), sc_f, sh_f, ffn_w_in[i].astype(BF16),
                 ffn_w_out[i].astype(BF16), vec(ffn_post_g[i]), gt_f)
    return x
```

```python
import functools
import math

import jax
import jax.numpy as jnp
from jax import lax
from jax.experimental import pallas as pl
from jax.experimental.pallas import tpu as pltpu

F32 = jnp.float32
BF16 = jnp.bfloat16
EPS = 1e-6
HIGHEST = lax.Precision.HIGHEST

LANES = 128
SUBLANES = 8
VMEM_LIMIT = 56 * 1024 * 1024

SSD_HEAD_DIM = 64
SSD_N_HEADS = 16
SSD_N_GROUPS = 2
SSD_D_STATE = 128
SSD_CONV = 4
SSD_CHUNK = 128
S5_GROUP = 16
S5_STATE = 64
ATT_HEAD_DIM = 128
ATT_PATTERNS = ((128, 1), (512, 4), (2048, 16))
ATT_BLOCK = 128
ATT_UNIT = ATT_BLOCK * 16


def _cparams(*sem):
    return pltpu.CompilerParams(dimension_semantics=sem, vmem_limit_bytes=VMEM_LIMIT)


def _resident(shape):
    nd = len(shape)
    return pl.BlockSpec(shape, lambda *_: (0,) * nd, pipeline_mode=pl.Buffered(1))


def _silu(v):
    return v * jax.nn.sigmoid(v)


def _normmod(x, g, sc, sh):
    ms = jnp.mean(x * x, axis=-1, keepdims=True)
    return (x * lax.rsqrt(ms + EPS) * g) * (1.0 + sc) + sh


def _post_residual(x, y, g, gt):
    ms = jnp.mean(y * y, axis=-1, keepdims=True)
    return x + gt * (y * lax.rsqrt(ms + EPS) * g)


def _ada_kernel(c_ref, w_ref, b_ref, o_ref):
    cond = _silu(c_ref[...])
    o_ref[...] = jnp.dot(cond, w_ref[...], precision=HIGHEST,
                         preferred_element_type=F32) + b_ref[...]


def _ada(c, ada_w, ada_b):
    depth, d, n = ada_w.shape
    b = c.shape[0]
    rows = -(-b // SUBLANES) * SUBLANES
    c_pad = jnp.zeros((rows, d), F32).at[:b].set(c)
    tn = 1536
    out = pl.pallas_call(
        _ada_kernel,
        grid=(depth, n // tn),
        in_specs=[pl.BlockSpec((rows, d), lambda i, j: (0, 0)),
                  pl.BlockSpec((None, d, tn), lambda i, j: (i, 0, j)),
                  pl.BlockSpec((None, 1, tn), lambda i, j: (i, 0, j))],
        out_specs=pl.BlockSpec((None, rows, tn), lambda i, j: (i, 0, j)),
        out_shape=jax.ShapeDtypeStruct((depth, rows, n), F32),
        compiler_params=_cparams("arbitrary", "arbitrary"),
    )(c_pad, ada_w, ada_b.reshape(depth, 1, n))
    return out[:, :b]


def _inproj_kernel(x_ref, g_ref, sc_ref, sh_ref, w_ref, *o_refs, splits, slabbed):
    h = _normmod(x_ref[...], g_ref[...], sc_ref[...], sh_ref[...]).astype(BF16)
    off = 0
    for o_ref, n, slab in zip(o_refs, splits, slabbed):
        r = jnp.dot(h, w_ref[:, off:off + n], preferred_element_type=F32)
        if slab:
            for k in range(n // LANES):
                o_ref[k] = r[:, k * LANES:(k + 1) * LANES]
        else:
            o_ref[...] = r
        off += n


def _inproj(x, g, sc, sh, w, splits, slabbed, tm=512):
    b, l, d = x.shape
    row = lambda bi, i: (bi, i, 0)
    per_b = lambda bi, i: (bi, 0, 0)
    slab_row = lambda bi, i: (bi, 0, i, 0)
    return pl.pallas_call(
        functools.partial(_inproj_kernel, splits=splits, slabbed=slabbed),
        grid=(b, l // tm),
        in_specs=[pl.BlockSpec((None, tm, d), row),
                  _resident((1, d)),
                  pl.BlockSpec((None, 1, d), per_b),
                  pl.BlockSpec((None, 1, d), per_b),
                  _resident(w.shape)],
        out_specs=[pl.BlockSpec((None, n // LANES, tm, LANES), slab_row) if s
                   else pl.BlockSpec((None, tm, n), row) for n, s in zip(splits, slabbed)],
        out_shape=[jax.ShapeDtypeStruct((b, n // LANES, l, LANES) if s else (b, l, n), F32)
                   for n, s in zip(splits, slabbed)],
        compiler_params=_cparams("parallel", "parallel"),
    )(x, g, sc, sh, w)


def _qkvproj_kernel(x_ref, g_ref, sc_ref, sh_ref, w_ref, o_ref, *, n_heads):
    h = _normmod(x_ref[...], g_ref[...], sc_ref[...], sh_ref[...]).astype(BF16)
    group = 8
    for h0 in range(0, n_heads, group):
        r = jnp.dot(h, w_ref[:, h0 * ATT_HEAD_DIM:(h0 + group) * ATT_HEAD_DIM],
                    preferred_element_type=F32)
        for k in range(group):
            o_ref[h0 + k] = r[:, k * ATT_HEAD_DIM:(k + 1) * ATT_HEAD_DIM]


def _qkvproj(x, g, sc, sh, w, tm=512):
    b, l, d = x.shape
    n_heads = w.shape[1] // ATT_HEAD_DIM
    per_b = lambda bi, i: (bi, 0, 0)
    return pl.pallas_call(
        functools.partial(_qkvproj_kernel, n_heads=n_heads),
        grid=(b, l // tm),
        in_specs=[pl.BlockSpec((None, tm, d), lambda bi, i: (bi, i, 0)),
                  _resident((1, d)),
                  pl.BlockSpec((None, 1, d), per_b),
                  pl.BlockSpec((None, 1, d), per_b),
                  _resident(w.shape)],
        out_specs=pl.BlockSpec((None, n_heads, tm, ATT_HEAD_DIM), lambda bi, i: (bi, 0, i, 0)),
        out_shape=jax.ShapeDtypeStruct((b, n_heads, l, ATT_HEAD_DIM), F32),
        compiler_params=_cparams("parallel", "parallel"),
    )(x, g, sc, sh, w)


def _outproj_kernel(*refs, n_in):
    acts = refs[:n_in]
    ws = refs[n_in:2 * n_in]
    x_ref, g_ref, gt_ref, o_ref = refs[2 * n_in:]
    y = None
    for a_ref, w_ref in zip(acts, ws):
        if len(a_ref.shape) == 3:
            a = jnp.concatenate([a_ref[k] for k in range(a_ref.shape[0])], axis=1)
        else:
            a = a_ref[...]
        t = jnp.dot(a.astype(BF16), w_ref[...], preferred_element_type=F32)
        y = t if y is None else y + t
    o_ref[...] = _post_residual(x_ref[...], y, g_ref[...], gt_ref[...])


def _outproj(acts, ws, x, g, gt, tm=512):
    b, l, d = x.shape
    row = lambda bi, i: (bi, i, 0)
    n_in = len(acts)
    act_spec = lambda a: (pl.BlockSpec((None, tm, a.shape[-1]), row) if a.ndim == 3 else
                          pl.BlockSpec((None, a.shape[1], tm, LANES), lambda bi, i: (bi, 0, i, 0)))
    return pl.pallas_call(
        functools.partial(_outproj_kernel, n_in=n_in),
        grid=(b, l // tm),
        in_specs=([act_spec(a) for a in acts]
                  + [_resident(w.shape) for w in ws]
                  + [pl.BlockSpec((None, tm, d), row),
                     _resident((1, d)),
                     pl.BlockSpec((None, 1, d), lambda bi, i: (bi, 0, 0))]),
        out_specs=pl.BlockSpec((None, tm, d), row),
        out_shape=jax.ShapeDtypeStruct((b, l, d), F32),
        compiler_params=_cparams("parallel", "parallel"),
    )(*acts, *ws, x, g, gt)


def _ffn_kernel(x_ref, gpre_ref, sc_ref, sh_ref, win_ref, wout_ref, gpost_ref, gt_ref,
                o_ref, act_ref, *, hidden, chunk):
    x = x_ref[...]
    h = _normmod(x, gpre_ref[...], sc_ref[...], sh_ref[...]).astype(BF16)
    for c0 in range(0, hidden, chunk):
        gate = jnp.dot(h, win_ref[:, c0:c0 + chunk], preferred_element_type=F32)
        up = jnp.dot(h, win_ref[:, hidden + c0:hidden + c0 + chunk], preferred_element_type=F32)
        act_ref[:, c0:c0 + chunk] = (_silu(gate) * up).astype(BF16)
    y = jnp.dot(act_ref[...], wout_ref[...], preferred_element_type=F32)
    o_ref[...] = _post_residual(x, y, gpost_ref[...], gt_ref[...])


def _ffn(x, gpre, sc, sh, win, wout, gpost, gt, tm=512, chunk=256):
    b, l, d = x.shape
    hidden = wout.shape[0]
    row = lambda bi, i: (bi, i, 0)
    per_b = lambda bi, i: (bi, 0, 0)
    return pl.pallas_call(
        functools.partial(_ffn_kernel, hidden=hidden, chunk=chunk),
        grid=(b, l // tm),
        in_specs=[pl.BlockSpec((None, tm, d), row),
                  _resident((1, d)),
                  pl.BlockSpec((None, 1, d), per_b),
                  pl.BlockSpec((None, 1, d), per_b),
                  _resident(win.shape),
                  _resident(wout.shape),
                  _resident((1, d)),
                  pl.BlockSpec((None, 1, d), per_b)],
        out_specs=pl.BlockSpec((None, tm, d), row),
        out_shape=jax.ShapeDtypeStruct((b, l, d), F32),
        scratch_shapes=[pltpu.VMEM((tm, hidden), BF16)],
        compiler_params=_cparams("parallel", "parallel"),
    )(x, gpre, sc, sh, win, wout, gpost, gt)


def _split2(v):
    hi = v.astype(BF16)
    return hi, (v - hi.astype(F32)).astype(BF16)


def _split3(v):
    hi = v.astype(BF16)
    r = v - hi.astype(F32)
    mid = r.astype(BF16)
    return hi, mid, (r - mid.astype(F32)).astype(BF16)


def _expand(parts, e):
    out = None
    for p in parts:
        t = jnp.dot(p, e, preferred_element_type=F32)
        out = t if out is None else out + t
    return out


def _ssd_kernel(z_ref, xbc_ref, dt_ref, cw_ref, cb_ref, dtb_ref, alog_ref, dsk_ref, ng_ref,
                e64_ref, e128_ref, o_ref, ext_ref, h_ref, *, ts):
    t = SSD_CHUNK
    d_inner = SSD_N_HEADS * SSD_HEAD_DIM
    gw = d_inner // SSD_N_GROUPS
    hpg = SSD_N_HEADS // SSD_N_GROUPS
    pad = SUBLANES

    @pl.when(pl.program_id(1) == 0)
    def _():
        ext_ref[0:pad, :] = jnp.zeros((pad, ext_ref.shape[1]), F32)
        h_ref[...] = jnp.zeros(h_ref.shape, F32)

    ext_ref[pad:pad + ts, :] = xbc_ref[...]

    li = lax.broadcasted_iota(jnp.int32, (t, t), 0)
    si = lax.broadcasted_iota(jnp.int32, (t, t), 1)
    causal = li >= si
    tril = causal.astype(F32)
    lane = lax.broadcasted_iota(jnp.int32, (t, LANES), 1)
    low_half = lane < SSD_HEAD_DIM
    a_neg = -jnp.exp(alog_ref[...])
    e64 = e64_ref[...]
    e128 = e128_ref[...]

    for c in range(ts // t):
        r0 = c * t
        acc = cb_ref[...] + cw_ref[0:1, :] * ext_ref[pad + r0 - 3:pad + r0 - 3 + t, :]
        for k in range(1, SSD_CONV):
            acc = acc + cw_ref[k:k + 1, :] * ext_ref[pad + r0 - 3 + k:pad + r0 - 3 + k + t, :]
        xc = _silu(acc)
        xs = xc[:, :d_inner]
        dt = jax.nn.softplus(dt_ref[r0:r0 + t, :] + dtb_ref[...])
        ac = jnp.dot(tril, dt * a_neg, precision=HIGHEST, preferred_element_type=F32)
        ac_t = ac.T
        ac_last = ac[t - 1:t, :]
        dt_e = _expand(_split2(dt), e64)
        sdec_e = _expand(_split2(dt * jnp.exp(ac_last - ac)), e64)
        eac_e = _expand(_split2(jnp.exp(ac)), e64)
        cdec_e = _expand(_split2(jnp.broadcast_to(jnp.exp(ac_last), (SUBLANES, LANES))), e64)[0:1, :]
        ac_e = _expand(_split3(ac), e128)
        x_dt = (xs * dt_e).astype(BF16)
        x_st = (xs * sdec_e).astype(BF16)
        ys = []
        for g in range(SSD_N_GROUPS):
            bm = xc[:, d_inner + g * SSD_D_STATE:d_inner + (g + 1) * SSD_D_STATE]
            cm = xc[:, d_inner + (SSD_N_GROUPS + g) * SSD_D_STATE:
                    d_inner + (SSD_N_GROUPS + g + 1) * SSD_D_STATE].astype(BF16)
            cb = lax.dot_general(cm, bm.astype(BF16), (((1,), (1,)), ((), ())),
                                 preferred_element_type=F32)
            gs = slice(g * gw, (g + 1) * gw)
            h_prev = h_ref[:, gs]
            y_off = jnp.dot(cm, h_prev.astype(BF16), preferred_element_type=F32) * eac_e[:, gs]
            s_new = jnp.dot(bm.T.astype(BF16), x_st[:, gs], preferred_element_type=F32)
            h_ref[:, gs] = h_prev * cdec_e[:, gs] + s_new
            pairs = []
            for q in range(hpg // 2):
                ps = []
                for k in range(2):
                    hh = g * hpg + 2 * q + k
                    diff = ac_e[:, hh * LANES:(hh + 1) * LANES] - ac_t[hh:hh + 1, :]
                    decay = jnp.where(causal, jnp.exp(jnp.minimum(diff, 0.0)), 0.0)
                    ps.append((cb * decay).astype(BF16))
                xp = x_dt[:, g * gw + q * LANES:g * gw + (q + 1) * LANES]
                zero = jnp.zeros_like(xp)
                lhs = jnp.concatenate(ps, axis=1)
                rhs = jnp.concatenate([jnp.where(low_half, xp, zero),
                                       jnp.where(low_half, zero, xp)], axis=0)
                pairs.append(jnp.dot(lhs, rhs, preferred_element_type=F32))
            ys.append(jnp.concatenate(pairs, axis=1) + y_off)
        y = jnp.concatenate(ys, axis=1) + dsk_ref[...] * xs
        y = y * _silu(z_ref[r0:r0 + t, :])
        outs = []
        for g in range(SSD_N_GROUPS):
            yg = y[:, g * gw:(g + 1) * gw]
            outs.append(yg * lax.rsqrt(jnp.mean(yg * yg, axis=-1, keepdims=True) + EPS))
        o_ref[r0:r0 + t, :] = jnp.concatenate(outs, axis=1) * ng_ref[...]

    ext_ref[0:pad, :] = ext_ref[ts:ts + pad, :]


def _ssd(z, xbc, dt, conv_w, conv_b, dt_bias, a_log, d_skip, norm_g, ts=256):
    b, l, d_inner = z.shape
    cdim = xbc.shape[-1]
    nh = SSD_N_HEADS
    pad1 = lambda v: jnp.zeros((1, LANES), F32).at[0, :nh].set(v)
    head = jnp.arange(LANES)[:, None]
    e64 = (head == (jnp.arange(d_inner) // SSD_HEAD_DIM)[None, :]) & (head < nh)
    e128 = (head == (jnp.arange(nh * LANES) // LANES)[None, :]) & (head < nh)
    row = lambda bi, i: (bi, i, 0)
    return pl.pallas_call(
        functools.partial(_ssd_kernel, ts=ts),
        grid=(b, l // ts),
        in_specs=[pl.BlockSpec((None, ts, d_inner), row),
                  pl.BlockSpec((None, ts, cdim), row),
                  pl.BlockSpec((None, ts, LANES), row),
                  _resident(conv_w.shape),
                  _resident((1, cdim)),
                  _resident((1, LANES)),
                  _resident((1, LANES)),
                  _resident((1, d_inner)),
                  _resident((1, d_inner)),
                  _resident((LANES, d_inner)),
                  _resident((LANES, nh * LANES))],
        out_specs=pl.BlockSpec((None, ts, d_inner), row),
        out_shape=jax.ShapeDtypeStruct((b, l, d_inner), F32),
        scratch_shapes=[pltpu.VMEM((SUBLANES + ts, cdim), F32),
                        pltpu.VMEM((SSD_D_STATE, d_inner), F32)],
        compiler_params=_cparams("parallel", "arbitrary"),
    )(z, xbc, dt, conv_w, conv_b.reshape(1, cdim), pad1(dt_bias), pad1(a_log),
      jnp.repeat(d_skip, SSD_HEAD_DIM).reshape(1, d_inner), norm_g.reshape(1, d_inner),
      e64.astype(BF16), e128.astype(BF16))


def _s5_kernel(u_ref, bmat_ref, cmat_ref, are_ref, aim_ref, asre_ref, asim_ref, dsk_ref,
               gluw_ref, glub_ref, o_ref, pad_ref, up_ref, hre_ref, him_ref,
               cre_ref, cim_ref, y_ref, *, tile):
    seg = tile // SUBLANES
    segp = seg + SUBLANES
    n_slab = u_ref.shape[0]
    sw = hre_ref.shape[-1] // n_slab

    @pl.when(pl.program_id(1) == 0)
    def _():
        cre_ref[...] = jnp.zeros(cre_ref.shape, F32)
        cim_ref[...] = jnp.zeros(cim_ref.shape, F32)

    for j in range(n_slab):
        for k in range(SUBLANES):
            pad_ref[j, k * segp:k * segp + seg, :] = u_ref[j, k * seg:(k + 1) * seg, :]
    for i in range(seg):
        for j in range(n_slab):
            up_ref[i * SUBLANES:(i + 1) * SUBLANES, j * LANES:(j + 1) * LANES] = (
                pad_ref[j, pl.ds(i, SUBLANES, stride=segp), :])

    def in_map(j):
        bu = jnp.dot(up_ref[:, j * LANES:(j + 1) * LANES].astype(BF16), bmat_ref[j],
                     preferred_element_type=F32)
        hre_ref[:, j * sw:(j + 1) * sw] = bu[:, :sw]
        him_ref[:, j * sw:(j + 1) * sw] = bu[:, sw:]

    def scan(j, s_re, s_im, store):
        cols = slice(j * sw, (j + 1) * sw)
        a_re = jnp.broadcast_to(are_ref[:, cols], (SUBLANES, sw))
        a_im = jnp.broadcast_to(aim_ref[:, cols], (SUBLANES, sw))
        for i in range(seg):
            rows = slice(i * SUBLANES, (i + 1) * SUBLANES)
            s_re, s_im = (a_re * s_re - a_im * s_im + hre_ref[rows, cols],
                          a_re * s_im + a_im * s_re + him_ref[rows, cols])
            if store:
                hre_ref[rows, cols] = s_re
                him_ref[rows, cols] = s_im
        return s_re, s_im

    def out_map(j):
        cols = slice(j * sw, (j + 1) * sw)
        ch = slice(j * LANES, (j + 1) * LANES)
        y = (jnp.dot(hre_ref[:, cols].astype(BF16), cmat_ref[j, :sw, :], preferred_element_type=F32)
             + jnp.dot(him_ref[:, cols].astype(BF16), cmat_ref[j, sw:, :], preferred_element_type=F32))
        y = y + dsk_ref[:, ch] * up_ref[:, ch]
        y_ref[:, ch] = 0.5 * y * (1.0 + lax.erf(y * (2.0 ** -0.5)))

    zeros = jnp.zeros((SUBLANES, sw), F32)
    in_map(0)
    for j in range(n_slab):
        if j + 1 < n_slab:
            in_map(j + 1)
        cols = slice(j * sw, (j + 1) * sw)
        e_re, e_im = scan(j, zeros, zeros, store=False)
        s_re, s_im = cre_ref[:, cols], cim_ref[:, cols]
        as_re, as_im = asre_ref[:, cols], asim_ref[:, cols]
        in_re, in_im = [], []
        for k in range(SUBLANES):
            in_re.append(s_re)
            in_im.append(s_im)
            s_re, s_im = (as_re * s_re - as_im * s_im + e_re[k:k + 1, :],
                          as_re * s_im + as_im * s_re + e_im[k:k + 1, :])
        cre_ref[:, cols] = s_re
        cim_ref[:, cols] = s_im
        scan(j, jnp.concatenate(in_re, axis=0), jnp.concatenate(in_im, axis=0), store=True)
        out_map(j)

    y = y_ref[...]
    gate = jnp.dot(y.astype(BF16), gluw_ref[...], preferred_element_type=F32) + glub_ref[...]
    y_ref[...] = y * jax.nn.sigmoid(gate)

    for i in range(seg):
        for j in range(n_slab):
            pad_ref[j, pl.ds(i, SUBLANES, stride=segp), :] = (
                y_ref[i * SUBLANES:(i + 1) * SUBLANES, j * LANES:(j + 1) * LANES])
    for j in range(n_slab):
        for k in range(SUBLANES):
            o_ref[j, k * seg:(k + 1) * seg, :] = pad_ref[j, k * segp:k * segp + seg, :]


def _s5_params(lam_re, lam_im, log_dt, b_re, b_im, c_re, c_im, seg):
    ng, ns = lam_re.shape
    gps = LANES // S5_GROUP
    n_slab = ng // gps
    dt = jnp.exp(log_dt)[:, None]
    mag = jnp.exp(lam_re * dt)
    a_re, a_im = mag * jnp.cos(lam_im * dt), mag * jnp.sin(lam_im * dt)
    den = lam_re * lam_re + lam_im * lam_im
    q_re = ((a_re - 1.0) * lam_re + a_im * lam_im) / den
    q_im = (a_im * lam_re - (a_re - 1.0) * lam_im) / den
    bb_re = q_re[..., None] * b_re - q_im[..., None] * b_im
    bb_im = q_re[..., None] * b_im + q_im[..., None] * b_re
    eye = jnp.eye(gps, dtype=F32)

    def in_map(bb):
        t = bb.reshape(n_slab, gps, ns, S5_GROUP)
        return jnp.einsum('jgpc,gh->jgchp', t, eye).reshape(n_slab, LANES, gps * ns)

    def out_map(cc):
        t = cc.reshape(n_slab, gps, S5_GROUP, ns)
        return jnp.einsum('jgcp,gh->jgphc', t, eye).reshape(n_slab, gps * ns, LANES)

    bmat = jnp.concatenate([in_map(bb_re), in_map(bb_im)], axis=2).astype(BF16)
    cmat = jnp.concatenate([out_map(c_re), -out_map(c_im)], axis=1).astype(BF16)
    as_re, as_im = a_re, a_im
    for _ in range(int(math.log2(seg))):
        as_re, as_im = as_re * as_re - as_im * as_im, 2.0 * as_re * as_im
    flat = lambda v: v.reshape(1, ng * ns)
    return bmat, cmat, flat(a_re), flat(a_im), flat(as_re), flat(as_im)


def _s5(u, lam_re, lam_im, log_dt, b_re, b_im, c_re, c_im, d_skip, glu_w, glu_b, tile=256):
    b, n_slab, l, _ = u.shape
    width = n_slab * LANES
    seg = tile // SUBLANES
    assert seg & (seg - 1) == 0
    bmat, cmat, a_re, a_im, as_re, as_im = _s5_params(lam_re, lam_im, log_dt, b_re, b_im,
                                                      c_re, c_im, seg)
    n_state = a_re.shape[-1]
    row = lambda bi, i: (bi, 0, i, 0)
    vec = _resident((1, n_state))
    return pl.pallas_call(
        functools.partial(_s5_kernel, tile=tile),
        grid=(b, l // tile),
        in_specs=[pl.BlockSpec((None, n_slab, tile, LANES), row),
                  _resident(bmat.shape), _resident(cmat.shape),
                  vec, vec, vec, vec,
                  _resident((1, width)),
                  _resident(glu_w.shape),
                  _resident((1, width))],
        out_specs=pl.BlockSpec((None, n_slab, tile, LANES), row),
        out_shape=jax.ShapeDtypeStruct((b, n_slab, l, LANES), F32),
        scratch_shapes=[pltpu.VMEM((n_slab, tile + SUBLANES * SUBLANES, LANES), F32),
                        pltpu.VMEM((tile, width), F32),
                        pltpu.VMEM((tile, n_state), F32),
                        pltpu.VMEM((tile, n_state), F32),
                        pltpu.VMEM((1, n_state), F32),
                        pltpu.VMEM((1, n_state), F32),
                        pltpu.VMEM((tile, width), F32)],
        compiler_params=_cparams("parallel", "arbitrary"),
    )(u, bmat, cmat, a_re, a_im, as_re, as_im, d_skip.reshape(1, width),
      glu_w.astype(BF16), glu_b.reshape(1, width))


def _attn_kernel(q0_ref, q1_ref, q2_ref, k_ref, v_ref, o_ref,
                 kb0, vb0, kb1, vb1, kb2, vb2, acc_ref, lse_ref):
    blk = ATT_BLOCK
    scale = ATT_HEAD_DIM ** -0.5
    first = pl.program_id(2) == 0
    qi = lax.broadcasted_iota(jnp.int32, (blk, 2 * blk), 0)
    cj = lax.broadcasted_iota(jnp.int32, (blk, 2 * blk), 1)
    band = (cj >= qi) & (cj <= qi + blk)
    band_first = band & (cj >= jnp.where(first, blk, 0))
    neg = jnp.float32(-jnp.inf)
    dn = (((1,), (1,)), ((), ()))
    bufs = ((kb0, vb0), (kb1, vb1), (kb2, vb2))

    @pl.when(first)
    def _():
        for kb, vb in bufs:
            kb[...] = jnp.zeros(kb.shape, BF16)
            vb[...] = jnp.zeros(vb.shape, BF16)

    for (kb, vb), (window, dil) in zip(bufs, ATT_PATTERNS):
        m = ATT_UNIT // dil
        for r in range(dil):
            src = pl.ds(r, m, stride=dil) if dil > 1 else pl.ds(0, m)
            dst = slice(r * (m + blk) + blk, (r + 1) * (m + blk))
            kb[dst, :] = k_ref[src, :].astype(BF16)
            vb[dst, :] = v_ref[src, :].astype(BF16)

    for p, (q_ref, (kb, vb), (window, dil)) in enumerate(
            zip((q0_ref, q1_ref, q2_ref), bufs, ATT_PATTERNS)):
        assert window // dil == blk
        m = ATT_UNIT // dil
        for r in range(dil):
            for bi in range(m // blk):
                start = r + blk * dil * bi
                rows = pl.ds(start, blk, stride=dil) if dil > 1 else pl.ds(start, blk)
                w0 = r * (m + blk) + blk * bi
                win = slice(w0, w0 + 2 * blk)
                q = (q_ref[rows, :] * scale).astype(BF16)
                s = lax.dot_general(q, kb[win, :], dn, preferred_element_type=F32)
                s = jnp.where(band_first if bi == 0 else band, s, neg)
                mx = jnp.max(s, axis=-1, keepdims=True)
                e = jnp.exp(s - mx)
                den = jnp.sum(e, axis=-1, keepdims=True)
                o = jnp.dot(e.astype(BF16), vb[win, :], preferred_element_type=F32)
                acc_ref[p, rows, :] = o / den
                lse_ref[p, rows, :] = jnp.broadcast_to(mx + jnp.log(den), (blk, ATT_HEAD_DIM))

    for (kb, vb), (window, dil) in zip(bufs, ATT_PATTERNS):
        m = ATT_UNIT // dil
        for r in range(dil):
            base = r * (m + blk)
            kb[base:base + blk, :] = kb[base + m:base + m + blk, :]
            vb[base:base + blk, :] = vb[base + m:base + m + blk, :]

    l0, l1, l2 = lse_ref[0], lse_ref[1], lse_ref[2]
    mx = jnp.maximum(jnp.maximum(l0, l1), l2)
    w0, w1, w2 = jnp.exp(l0 - mx), jnp.exp(l1 - mx), jnp.exp(l2 - mx)
    o_ref[...] = (w0 * acc_ref[0] + w1 * acc_ref[1] + w2 * acc_ref[2]) / (w0 + w1 + w2)


def _attention(qkv, n_kv):
    b, _, l, e = qkv.shape
    n_pat = len(ATT_PATTERNS)
    unit = ATT_UNIT
    tile = (None, None, unit, e)
    head = lambda off: pl.BlockSpec(tile, lambda bi, h, u, off=off: (bi, off + h, u, 0))
    staged = []
    for _, dil in ATT_PATTERNS:
        rows = dil * (unit // dil + ATT_BLOCK)
        staged += [pltpu.VMEM((rows, e), BF16), pltpu.VMEM((rows, e), BF16)]
    return pl.pallas_call(
        _attn_kernel,
        grid=(b, n_kv, l // unit),
        in_specs=[head(p * n_kv) for p in range(n_pat)] + [head(n_pat * n_kv),
                                                           head((n_pat + 1) * n_kv)],
        out_specs=pl.BlockSpec((None, unit, e), lambda bi, h, u: (bi, u, h)),
        out_shape=jax.ShapeDtypeStruct((b, l, n_kv * e), F32),
        scratch_shapes=staged + [pltpu.VMEM((n_pat, unit, e), F32),
                                 pltpu.VMEM((n_pat, unit, e), F32)],
        compiler_params=_cparams("parallel", "parallel", "arbitrary"),
    )(qkv, qkv, qkv, qkv, qkv)


def kernel(x, c, ada_w, ada_b, mix_pre_g, mix_post_g, ffn_pre_g, ffn_post_g, ffn_w_in, ffn_w_out, hyb_w_in, ssd_conv_w, ssd_conv_b, ssd_dt_bias, ssd_a_log, ssd_d, ssd_norm_g, s5_lambda_re, s5_lambda_im, s5_log_dt, s5_b_re, s5_b_im, s5_c_re, s5_c_im, s5_d, s5_glu_w, s5_glu_b, hyb_w_out, attn_w_qkv, attn_w_o):
    bsz, seq, d = x.shape
    depth = ada_w.shape[0]
    assert seq % ATT_UNIT == 0 and d == SSD_N_HEADS * SSD_HEAD_DIM
    d_inner = d
    conv_dim = ssd_conv_w.shape[-1]
    n_kv = d // ATT_HEAD_DIM

    mod = _ada(c, ada_w, ada_b)
    vec = lambda v: v.reshape(1, d)

    for i in range(depth):
        sh_m, sc_m, gt_m, sh_f, sc_f, gt_f = (
            mod[i, :, k * d:(k + 1) * d].reshape(bsz, 1, d) for k in range(6))
        j = i // 2
        if i % 2 == 0:
            w = hyb_w_in[j]
            o_dt = d_inner + conv_dim
            w_dt = jnp.zeros((d, LANES), F32).at[:, :SSD_N_HEADS].set(w[:, o_dt:o_dt + SSD_N_HEADS])
            w_cat = jnp.concatenate([w[:, :o_dt], w[:, o_dt + SSD_N_HEADS:], w_dt], axis=1).astype(BF16)
            z, xbc, u, dt = _inproj(x, vec(mix_pre_g[i]), sc_m, sh_m, w_cat,
                                    (d_inner, conv_dim, d, LANES), (False, False, True, False))
            y_ssd = _ssd(z, xbc, dt, ssd_conv_w[j], ssd_conv_b[j], ssd_dt_bias[j], ssd_a_log[j],
                         ssd_d[j], ssd_norm_g[j])
            y_s5 = _s5(u, s5_lambda_re[j], s5_lambda_im[j], s5_log_dt[j], s5_b_re[j], s5_b_im[j],
                       s5_c_re[j], s5_c_im[j], s5_d[j], s5_glu_w[j], s5_glu_b[j])
            w_out = hyb_w_out[j].astype(BF16)
            x = _outproj([y_ssd, y_s5], [w_out[:d_inner], w_out[d_inner:]], x,
                         vec(mix_post_g[i]), gt_m)
        else:
            qkv = _qkvproj(x, vec(mix_pre_g[i]), sc_m, sh_m, attn_w_qkv[j].astype(BF16))
            o = _attention(qkv, n_kv)
            x = _outproj([o], [attn_w_o[j].astype(BF16)], x, vec(mix_post_g[i]), gt_m)
        x = _ffn(x, vec(ffn_pre_g[i]), sc_f, sh_f, ffn_w_in[i].astype(BF16),
                 ffn_w_out[i].astype(BF16), vec(ffn_post_g[i]), gt_f)
    return x
```

```python
import functools
import math

import jax
import jax.numpy as jnp
from jax import lax
from jax.experimental import pallas as pl
from jax.experimental.pallas import tpu as pltpu

F32 = jnp.float32
BF16 = jnp.bfloat16
EPS = 1e-6
HIGHEST = lax.Precision.HIGHEST

LANES = 128
SUBLANES = 8
VMEM_LIMIT = 56 * 1024 * 1024

SSD_HEAD_DIM = 64
SSD_N_HEADS = 16
SSD_N_GROUPS = 2
SSD_D_STATE = 128
SSD_CONV = 4
SSD_CHUNK = 128
S5_GROUP = 16
S5_STATE = 64
ATT_HEAD_DIM = 128
ATT_PATTERNS = ((128, 1), (512, 4), (2048, 16))
ATT_BLOCK = 128
ATT_UNIT = ATT_BLOCK * 16
ATT_QROWS = 128


def _cparams(*sem):
    return pltpu.CompilerParams(dimension_semantics=sem, vmem_limit_bytes=VMEM_LIMIT)


def _resident(shape):
    nd = len(shape)
    return pl.BlockSpec(shape, lambda *_: (0,) * nd, pipeline_mode=pl.Buffered(1))


def _silu(v):
    return v * jax.nn.sigmoid(v)


def _normmod(x, g, sc, sh):
    ms = jnp.mean(x * x, axis=-1, keepdims=True)
    return (x * lax.rsqrt(ms + EPS) * g) * (1.0 + sc) + sh


def _post_residual(x, y, g, gt):
    ms = jnp.mean(y * y, axis=-1, keepdims=True)
    return x + gt * (y * lax.rsqrt(ms + EPS) * g)


def _ada_kernel(c_ref, w_ref, b_ref, o_ref):
    cond = _silu(c_ref[...])
    o_ref[...] = jnp.dot(cond, w_ref[...], precision=HIGHEST,
                         preferred_element_type=F32) + b_ref[...]


def _ada(c, ada_w, ada_b):
    depth, d, n = ada_w.shape
    b = c.shape[0]
    rows = -(-b // SUBLANES) * SUBLANES
    c_pad = jnp.zeros((rows, d), F32).at[:b].set(c)
    tn = 1536
    out = pl.pallas_call(
        _ada_kernel,
        grid=(depth, n // tn),
        in_specs=[pl.BlockSpec((rows, d), lambda i, j: (0, 0)),
                  pl.BlockSpec((None, d, tn), lambda i, j: (i, 0, j)),
                  pl.BlockSpec((None, 1, tn), lambda i, j: (i, 0, j))],
        out_specs=pl.BlockSpec((None, rows, tn), lambda i, j: (i, 0, j)),
        out_shape=jax.ShapeDtypeStruct((depth, rows, n), F32),
        compiler_params=_cparams("arbitrary", "arbitrary"),
    )(c_pad, ada_w, ada_b.reshape(depth, 1, n))
    return out[:, :b]


def _inproj_kernel(x_ref, g_ref, sc_ref, sh_ref, w_ref, *o_refs, splits, slabbed):
    h = _normmod(x_ref[...], g_ref[...], sc_ref[...], sh_ref[...]).astype(BF16)
    off = 0
    for o_ref, n, slab in zip(o_refs, splits, slabbed):
        r = jnp.dot(h, w_ref[:, off:off + n], preferred_element_type=F32)
        if slab:
            for k in range(n // LANES):
                o_ref[k] = r[:, k * LANES:(k + 1) * LANES]
        else:
            o_ref[...] = r
        off += n


def _inproj(x, g, sc, sh, w, splits, slabbed, tm=512):
    b, l, d = x.shape
    row = lambda bi, i: (bi, i, 0)
    per_b = lambda bi, i: (bi, 0, 0)
    slab_row = lambda bi, i: (bi, 0, i, 0)
    return pl.pallas_call(
        functools.partial(_inproj_kernel, splits=splits, slabbed=slabbed),
        grid=(b, l // tm),
        in_specs=[pl.BlockSpec((None, tm, d), row),
                  _resident((1, d)),
                  pl.BlockSpec((None, 1, d), per_b),
                  pl.BlockSpec((None, 1, d), per_b),
                  _resident(w.shape)],
        out_specs=[pl.BlockSpec((None, n // LANES, tm, LANES), slab_row) if s
                   else pl.BlockSpec((None, tm, n), row) for n, s in zip(splits, slabbed)],
        out_shape=[jax.ShapeDtypeStruct((b, n // LANES, l, LANES) if s else (b, l, n), F32)
                   for n, s in zip(splits, slabbed)],
        compiler_params=_cparams("parallel", "parallel"),
    )(x, g, sc, sh, w)


def _qkvproj_kernel(x_ref, g_ref, sc_ref, sh_ref, w_ref, o_ref, *, n_heads):
    h = _normmod(x_ref[...], g_ref[...], sc_ref[...], sh_ref[...]).astype(BF16)
    group = 8
    for h0 in range(0, n_heads, group):
        r = jnp.dot(h, w_ref[:, h0 * ATT_HEAD_DIM:(h0 + group) * ATT_HEAD_DIM],
                    preferred_element_type=F32)
        for k in range(group):
            o_ref[h0 + k] = r[:, k * ATT_HEAD_DIM:(k + 1) * ATT_HEAD_DIM]


def _qkvproj(x, g, sc, sh, w, tm=512):
    b, l, d = x.shape
    n_heads = w.shape[1] // ATT_HEAD_DIM
    per_b = lambda bi, i: (bi, 0, 0)
    return pl.pallas_call(
        functools.partial(_qkvproj_kernel, n_heads=n_heads),
        grid=(b, l // tm),
        in_specs=[pl.BlockSpec((None, tm, d), lambda bi, i: (bi, i, 0)),
                  _resident((1, d)),
                  pl.BlockSpec((None, 1, d), per_b),
                  pl.BlockSpec((None, 1, d), per_b),
                  _resident(w.shape)],
        out_specs=pl.BlockSpec((None, n_heads, tm, ATT_HEAD_DIM), lambda bi, i: (bi, 0, i, 0)),
        out_shape=jax.ShapeDtypeStruct((b, n_heads, l, ATT_HEAD_DIM), F32),
        compiler_params=_cparams("parallel", "parallel"),
    )(x, g, sc, sh, w)


def _mixout_ffn_kernel(*refs, n_in, hidden, chunk):
    acts = refs[:n_in]
    ws = refs[n_in:2 * n_in]
    (x_ref, gmix_ref, gtm_ref, gpre_ref, sc_ref, sh_ref, win_ref, wout_ref, gpost_ref, gtf_ref,
     o_ref, act_ref) = refs[2 * n_in:]
    y = None
    for a_ref, w_ref in zip(acts, ws):
        if len(a_ref.shape) == 3:
            a = jnp.concatenate([a_ref[k] for k in range(a_ref.shape[0])], axis=1)
        else:
            a = a_ref[...]
        t = jnp.dot(a.astype(BF16), w_ref[...], preferred_element_type=F32)
        y = t if y is None else y + t
    x = _post_residual(x_ref[...], y, gmix_ref[...], gtm_ref[...])

    h = _normmod(x, gpre_ref[...], sc_ref[...], sh_ref[...]).astype(BF16)
    for c0 in range(0, hidden, chunk):
        gate = jnp.dot(h, win_ref[:, c0:c0 + chunk], preferred_element_type=F32)
        up = jnp.dot(h, win_ref[:, hidden + c0:hidden + c0 + chunk], preferred_element_type=F32)
        act_ref[:, c0:c0 + chunk] = (_silu(gate) * up).astype(BF16)
    y = jnp.dot(act_ref[...], wout_ref[...], preferred_element_type=F32)
    o_ref[...] = _post_residual(x, y, gpost_ref[...], gtf_ref[...])


def _mixout_ffn(acts, ws, x, gmix, gtm, gpre, sc, sh, win, wout, gpost, gtf, tm=512, chunk=256):
    b, l, d = x.shape
    hidden = wout.shape[0]
    row = lambda bi, i: (bi, i, 0)
    per_b = lambda bi, i: (bi, 0, 0)
    n_in = len(acts)
    act_spec = lambda a: (pl.BlockSpec((None, tm, a.shape[-1]), row) if a.ndim == 3 else
                          pl.BlockSpec((None, a.shape[1], tm, LANES), lambda bi, i: (bi, 0, i, 0)))
    vec = _resident((1, d))
    mod = pl.BlockSpec((None, 1, d), per_b)
    return pl.pallas_call(
        functools.partial(_mixout_ffn_kernel, n_in=n_in, hidden=hidden, chunk=chunk),
        grid=(b, l // tm),
        in_specs=([act_spec(a) for a in acts] + [_resident(w.shape) for w in ws]
                  + [pl.BlockSpec((None, tm, d), row), vec, mod, vec, mod, mod,
                     _resident(win.shape), _resident(wout.shape), vec, mod]),
        out_specs=pl.BlockSpec((None, tm, d), row),
        out_shape=jax.ShapeDtypeStruct((b, l, d), F32),
        scratch_shapes=[pltpu.VMEM((tm, hidden), BF16)],
        compiler_params=_cparams("parallel", "parallel"),
    )(*acts, *ws, x, gmix, gtm, gpre, sc, sh, win, wout, gpost, gtf)


def _split2(v):
    hi = v.astype(BF16)
    return hi, (v - hi.astype(F32)).astype(BF16)


def _split3(v):
    hi = v.astype(BF16)
    r = v - hi.astype(F32)
    mid = r.astype(BF16)
    return hi, mid, (r - mid.astype(F32)).astype(BF16)


def _expand(parts, e):
    out = None
    for p in parts:
        t = jnp.dot(p, e, preferred_element_type=F32)
        out = t if out is None else out + t
    return out


def _ssd_kernel(z_ref, xbc_ref, dt_ref, cw_ref, cb_ref, dtb_ref, alog_ref, dsk_ref, ng_ref,
                e64_ref, e128_ref, o_ref, ext_ref, h_ref, *, ts):
    t = SSD_CHUNK
    d_inner = SSD_N_HEADS * SSD_HEAD_DIM
    gw = d_inner // SSD_N_GROUPS
    hpg = SSD_N_HEADS // SSD_N_GROUPS
    pad = SUBLANES

    @pl.when(pl.program_id(1) == 0)
    def _():
        ext_ref[0:pad, :] = jnp.zeros((pad, ext_ref.shape[1]), F32)
        h_ref[...] = jnp.zeros(h_ref.shape, F32)

    ext_ref[pad:pad + ts, :] = xbc_ref[...]

    li = lax.broadcasted_iota(jnp.int32, (t, t), 0)
    si = lax.broadcasted_iota(jnp.int32, (t, t), 1)
    causal = li >= si
    tril = causal.astype(F32)
    lane = lax.broadcasted_iota(jnp.int32, (t, LANES), 1)
    low_half = lane < SSD_HEAD_DIM
    a_neg = -jnp.exp(alog_ref[...])
    e64 = e64_ref[...]
    e128 = e128_ref[...]

    for c in range(ts // t):
        r0 = c * t
        acc = cb_ref[...] + cw_ref[0:1, :] * ext_ref[pad + r0 - 3:pad + r0 - 3 + t, :]
        for k in range(1, SSD_CONV):
            acc = acc + cw_ref[k:k + 1, :] * ext_ref[pad + r0 - 3 + k:pad + r0 - 3 + k + t, :]
        xc = _silu(acc)
        xs = xc[:, :d_inner]
        dt = jax.nn.softplus(dt_ref[r0:r0 + t, :] + dtb_ref[...])
        ac = jnp.dot(tril, dt * a_neg, precision=HIGHEST, preferred_element_type=F32)
        ac_t = ac.T
        ac_last = ac[t - 1:t, :]
        dt_e = _expand(_split2(dt), e64)
        sdec_e = _expand(_split2(dt * jnp.exp(ac_last - ac)), e64)
        eac_e = _expand(_split2(jnp.exp(ac)), e64)
        cdec_e = _expand(_split2(jnp.broadcast_to(jnp.exp(ac_last), (SUBLANES, LANES))), e64)[0:1, :]
        ac_e = _expand(_split3(ac), e128)
        x_dt = (xs * dt_e).astype(BF16)
        x_st = (xs * sdec_e).astype(BF16)
        ys = []
        for g in range(SSD_N_GROUPS):
            bm = xc[:, d_inner + g * SSD_D_STATE:d_inner + (g + 1) * SSD_D_STATE]
            cm = xc[:, d_inner + (SSD_N_GROUPS + g) * SSD_D_STATE:
                    d_inner + (SSD_N_GROUPS + g + 1) * SSD_D_STATE].astype(BF16)
            cb = lax.dot_general(cm, bm.astype(BF16), (((1,), (1,)), ((), ())),
                                 preferred_element_type=F32)
            cb = jnp.where(causal, cb, 0.0)
            gs = slice(g * gw, (g + 1) * gw)
            h_prev = h_ref[:, gs]
            y_off = jnp.dot(cm, h_prev.astype(BF16), preferred_element_type=F32) * eac_e[:, gs]
            s_new = jnp.dot(bm.T.astype(BF16), x_st[:, gs], preferred_element_type=F32)
            h_ref[:, gs] = h_prev * cdec_e[:, gs] + s_new
            pairs = []
            for q in range(hpg // 2):
                ps = []
                for k in range(2):
                    hh = g * hpg + 2 * q + k
                    diff = ac_e[:, hh * LANES:(hh + 1) * LANES] - ac_t[hh:hh + 1, :]
                    ps.append((cb * jnp.exp(jnp.minimum(diff, 0.0))).astype(BF16))
                xp = x_dt[:, g * gw + q * LANES:g * gw + (q + 1) * LANES]
                zero = jnp.zeros_like(xp)
                lhs = jnp.concatenate(ps, axis=1)
                rhs = jnp.concatenate([jnp.where(low_half, xp, zero),
                                       jnp.where(low_half, zero, xp)], axis=0)
                pairs.append(jnp.dot(lhs, rhs, preferred_element_type=F32))
            ys.append(jnp.concatenate(pairs, axis=1) + y_off)
        y = jnp.concatenate(ys, axis=1) + dsk_ref[...] * xs
        y = y * _silu(z_ref[r0:r0 + t, :])
        outs = []
        for g in range(SSD_N_GROUPS):
            yg = y[:, g * gw:(g + 1) * gw]
            outs.append(yg * lax.rsqrt(jnp.mean(yg * yg, axis=-1, keepdims=True) + EPS))
        o_ref[r0:r0 + t, :] = jnp.concatenate(outs, axis=1) * ng_ref[...]

    ext_ref[0:pad, :] = ext_ref[ts:ts + pad, :]


def _ssd(z, xbc, dt, conv_w, conv_b, dt_bias, a_log, d_skip, norm_g, ts=256):
    b, l, d_inner = z.shape
    cdim = xbc.shape[-1]
    nh = SSD_N_HEADS
    pad1 = lambda v: jnp.zeros((1, LANES), F32).at[0, :nh].set(v)
    head = jnp.arange(LANES)[:, None]
    e64 = (head == (jnp.arange(d_inner) // SSD_HEAD_DIM)[None, :]) & (head < nh)
    e128 = (head == (jnp.arange(nh * LANES) // LANES)[None, :]) & (head < nh)
    row = lambda bi, i: (bi, i, 0)
    return pl.pallas_call(
        functools.partial(_ssd_kernel, ts=ts),
        grid=(b, l // ts),
        in_specs=[pl.BlockSpec((None, ts, d_inner), row),
                  pl.BlockSpec((None, ts, cdim), row),
                  pl.BlockSpec((None, ts, LANES), row),
                  _resident(conv_w.shape),
                  _resident((1, cdim)),
                  _resident((1, LANES)),
                  _resident((1, LANES)),
                  _resident((1, d_inner)),
                  _resident((1, d_inner)),
                  _resident((LANES, d_inner)),
                  _resident((LANES, nh * LANES))],
        out_specs=pl.BlockSpec((None, ts, d_inner), row),
        out_shape=jax.ShapeDtypeStruct((b, l, d_inner), F32),
        scratch_shapes=[pltpu.VMEM((SUBLANES + ts, cdim), F32),
                        pltpu.VMEM((SSD_D_STATE, d_inner), F32)],
        compiler_params=_cparams("parallel", "arbitrary"),
    )(z, xbc, dt, conv_w, conv_b.reshape(1, cdim), pad1(dt_bias), pad1(a_log),
      jnp.repeat(d_skip, SSD_HEAD_DIM).reshape(1, d_inner), norm_g.reshape(1, d_inner),
      e64.astype(BF16), e128.astype(BF16))


def _s5_kernel(u_ref, bmat_ref, cmat_ref, are_ref, aim_ref, asre_ref, asim_ref, dsk_ref,
               gluw_ref, glub_ref, o_ref, pad_ref, up_ref, hre_ref, him_ref,
               cre_ref, cim_ref, y_ref, *, tile):
    seg = tile // SUBLANES
    segp = seg + SUBLANES
    n_slab = u_ref.shape[0]
    sw = hre_ref.shape[-1] // n_slab

    @pl.when(pl.program_id(1) == 0)
    def _():
        cre_ref[...] = jnp.zeros(cre_ref.shape, F32)
        cim_ref[...] = jnp.zeros(cim_ref.shape, F32)

    for j in range(n_slab):
        for k in range(SUBLANES):
            pad_ref[j, k * segp:k * segp + seg, :] = u_ref[j, k * seg:(k + 1) * seg, :]
    for i in range(seg):
        for j in range(n_slab):
            up_ref[i * SUBLANES:(i + 1) * SUBLANES, j * LANES:(j + 1) * LANES] = (
                pad_ref[j, pl.ds(i, SUBLANES, stride=segp), :])

    def in_map(j):
        bu = jnp.dot(up_ref[:, j * LANES:(j + 1) * LANES].astype(BF16), bmat_ref[j],
                     preferred_element_type=F32)
        hre_ref[:, j * sw:(j + 1) * sw] = bu[:, :sw]
        him_ref[:, j * sw:(j + 1) * sw] = bu[:, sw:]

    def scan(j, s_re, s_im, store):
        cols = slice(j * sw, (j + 1) * sw)
        a_re = jnp.broadcast_to(are_ref[:, cols], (SUBLANES, sw))
        a_im = jnp.broadcast_to(aim_ref[:, cols], (SUBLANES, sw))
        for i in range(seg):
            rows = slice(i * SUBLANES, (i + 1) * SUBLANES)
            s_re, s_im = (a_re * s_re - a_im * s_im + hre_ref[rows, cols],
                          a_re * s_im + a_im * s_re + him_ref[rows, cols])
            if store:
                hre_ref[rows, cols] = s_re
                him_ref[rows, cols] = s_im
        return s_re, s_im

    def out_map(j):
        cols = slice(j * sw, (j + 1) * sw)
        ch = slice(j * LANES, (j + 1) * LANES)
        y = (jnp.dot(hre_ref[:, cols].astype(BF16), cmat_ref[j, :sw, :], preferred_element_type=F32)
             + jnp.dot(him_ref[:, cols].astype(BF16), cmat_ref[j, sw:, :], preferred_element_type=F32))
        y = y + dsk_ref[:, ch] * up_ref[:, ch]
        y_ref[:, ch] = 0.5 * y * (1.0 + lax.erf(y * (2.0 ** -0.5)))

    zeros = jnp.zeros((SUBLANES, sw), F32)
    in_map(0)
    for j in range(n_slab):
        if j + 1 < n_slab:
            in_map(j + 1)
        cols = slice(j * sw, (j + 1) * sw)
        e_re, e_im = scan(j, zeros, zeros, store=False)
        s_re, s_im = cre_ref[:, cols], cim_ref[:, cols]
        as_re, as_im = asre_ref[:, cols], asim_ref[:, cols]
        in_re, in_im = [], []
        for k in range(SUBLANES):
            in_re.append(s_re)
            in_im.append(s_im)
            s_re, s_im = (as_re * s_re - as_im * s_im + e_re[k:k + 1, :],
                          as_re * s_im + as_im * s_re + e_im[k:k + 1, :])
        cre_ref[:, cols] = s_re
        cim_ref[:, cols] = s_im
        scan(j, jnp.concatenate(in_re, axis=0), jnp.concatenate(in_im, axis=0), store=True)
        out_map(j)

    y = y_ref[...]
    gate = jnp.dot(y.astype(BF16), gluw_ref[...], preferred_element_type=F32) + glub_ref[...]
    y_ref[...] = y * jax.nn.sigmoid(gate)

    for i in range(seg):
        for j in range(n_slab):
            pad_ref[j, pl.ds(i, SUBLANES, stride=segp), :] = (
                y_ref[i * SUBLANES:(i + 1) * SUBLANES, j * LANES:(j + 1) * LANES])
    for j in range(n_slab):
        for k in range(SUBLANES):
            o_ref[j, k * seg:(k + 1) * seg, :] = pad_ref[j, k * segp:k * segp + seg, :]


def _s5_params(lam_re, lam_im, log_dt, b_re, b_im, c_re, c_im, seg):
    ng, ns = lam_re.shape
    gps = LANES // S5_GROUP
    n_slab = ng // gps
    dt = jnp.exp(log_dt)[:, None]
    mag = jnp.exp(lam_re * dt)
    a_re, a_im = mag * jnp.cos(lam_im * dt), mag * jnp.sin(lam_im * dt)
    den = lam_re * lam_re + lam_im * lam_im
    q_re = ((a_re - 1.0) * lam_re + a_im * lam_im) / den
    q_im = (a_im * lam_re - (a_re - 1.0) * lam_im) / den
    bb_re = q_re[..., None] * b_re - q_im[..., None] * b_im
    bb_im = q_re[..., None] * b_im + q_im[..., None] * b_re
    eye = jnp.eye(gps, dtype=F32)

    def in_map(bb):
        t = bb.reshape(n_slab, gps, ns, S5_GROUP)
        return jnp.einsum('jgpc,gh->jgchp', t, eye).reshape(n_slab, LANES, gps * ns)

    def out_map(cc):
        t = cc.reshape(n_slab, gps, S5_GROUP, ns)
        return jnp.einsum('jgcp,gh->jgphc', t, eye).reshape(n_slab, gps * ns, LANES)

    bmat = jnp.concatenate([in_map(bb_re), in_map(bb_im)], axis=2).astype(BF16)
    cmat = jnp.concatenate([out_map(c_re), -out_map(c_im)], axis=1).astype(BF16)
    as_re, as_im = a_re, a_im
    for _ in range(int(math.log2(seg))):
        as_re, as_im = as_re * as_re - as_im * as_im, 2.0 * as_re * as_im
    flat = lambda v: v.reshape(1, ng * ns)
    return bmat, cmat, flat(a_re), flat(a_im), flat(as_re), flat(as_im)


def _s5(u, lam_re, lam_im, log_dt, b_re, b_im, c_re, c_im, d_skip, glu_w, glu_b, tile=256):
    b, n_slab, l, _ = u.shape
    width = n_slab * LANES
    seg = tile // SUBLANES
    assert seg & (seg - 1) == 0
    bmat, cmat, a_re, a_im, as_re, as_im = _s5_params(lam_re, lam_im, log_dt, b_re, b_im,
                                                      c_re, c_im, seg)
    n_state = a_re.shape[-1]
    row = lambda bi, i: (bi, 0, i, 0)
    vec = _resident((1, n_state))
    return pl.pallas_call(
        functools.partial(_s5_kernel, tile=tile),
        grid=(b, l // tile),
        in_specs=[pl.BlockSpec((None, n_slab, tile, LANES), row),
                  _resident(bmat.shape), _resident(cmat.shape),
                  vec, vec, vec, vec,
                  _resident((1, width)),
                  _resident(glu_w.shape),
                  _resident((1, width))],
        out_specs=pl.BlockSpec((None, n_slab, tile, LANES), row),
        out_shape=jax.ShapeDtypeStruct((b, n_slab, l, LANES), F32),
        scratch_shapes=[pltpu.VMEM((n_slab, tile + SUBLANES * SUBLANES, LANES), F32),
                        pltpu.VMEM((tile, width), F32),
                        pltpu.VMEM((tile, n_state), F32),
                        pltpu.VMEM((tile, n_state), F32),
                        pltpu.VMEM((1, n_state), F32),
                        pltpu.VMEM((1, n_state), F32),
                        pltpu.VMEM((tile, width), F32)],
        compiler_params=_cparams("parallel", "arbitrary"),
    )(u, bmat, cmat, a_re, a_im, as_re, as_im, d_skip.reshape(1, width),
      glu_w.astype(BF16), glu_b.reshape(1, width))


def _attn_kernel(q0_ref, q1_ref, q2_ref, k_ref, v_ref, o_ref,
                 kb0, vb0, kb1, vb1, kb2, vb2, acc_ref, lse_ref, tk_ref, tv_ref):
    blk = ATT_BLOCK
    sub = ATT_QROWS
    scale = ATT_HEAD_DIM ** -0.5
    first = pl.program_id(2) == 0
    qi = lax.broadcasted_iota(jnp.int32, (blk, 2 * blk), 0)
    cj = lax.broadcasted_iota(jnp.int32, (blk, 2 * blk), 1)
    band = (cj >= qi) & (cj <= qi + blk)
    band_first = band & (cj >= jnp.where(first, blk, 0))
    neg = jnp.float32(-jnp.inf)
    dn = (((1,), (1,)), ((), ()))
    bufs = ((kb0, vb0), (kb1, vb1), (kb2, vb2))

    @pl.when(first)
    def _():
        for kb, vb in bufs:
            kb[...] = jnp.zeros(kb.shape, BF16)
            vb[...] = jnp.zeros(vb.shape, BF16)

    prev_dil = 1
    for (kb, vb), (window, dil) in zip(bufs, ATT_PATTERNS):
        m = ATT_UNIT // dil
        step = dil // prev_dil
        for r in range(dil):
            if dil == 1:
                src = pl.ds(0, m)
            else:
                src = pl.ds((r % prev_dil) * (ATT_UNIT // prev_dil) + r // prev_dil, m, stride=step)
            dst = slice(r * (m + blk) + blk, (r + 1) * (m + blk))
            for src_ref, dst_ref, tmp_ref in ((k_ref, kb, tk_ref), (v_ref, vb, tv_ref)):
                rows = (src_ref if prev_dil == 1 else tmp_ref)[src, :]
                dst_ref[dst, :] = rows.astype(BF16)
                if dil == ATT_PATTERNS[1][1]:
                    tmp_ref[r * m:(r + 1) * m, :] = rows
        prev_dil = dil

    for p, (q_ref, (kb, vb), (window, dil)) in enumerate(
            zip((q0_ref, q1_ref, q2_ref), bufs, ATT_PATTERNS)):
        assert window // dil == blk
        m = ATT_UNIT // dil
        for r in range(dil):
            for bi in range(m // blk):
                w0 = r * (m + blk) + blk * bi
                win = slice(w0, w0 + 2 * blk)
                valid = band_first if bi == 0 else band
                for q0 in range(0, blk, sub):
                    start = r + dil * (blk * bi + q0)
                    rows = pl.ds(start, sub, stride=dil) if dil > 1 else pl.ds(start, sub)
                    q = (q_ref[rows, :] * scale).astype(BF16)
                    s = lax.dot_general(q, kb[win, :], dn, preferred_element_type=F32)
                    s = jnp.where(valid[q0:q0 + sub], s, neg)
                    mx = jnp.max(s, axis=-1, keepdims=True)
                    e = jnp.exp(s - mx)
                    den = jnp.sum(e, axis=-1, keepdims=True)
                    o = jnp.dot(e.astype(BF16), vb[win, :], preferred_element_type=F32)
                    acc_ref[p, rows, :] = o / den
                    lse_ref[p, rows, :] = jnp.broadcast_to(mx + jnp.log(den), (sub, ATT_HEAD_DIM))

    for (kb, vb), (window, dil) in zip(bufs, ATT_PATTERNS):
        m = ATT_UNIT // dil
        for r in range(dil):
            base = r * (m + blk)
            kb[base:base + blk, :] = kb[base + m:base + m + blk, :]
            vb[base:base + blk, :] = vb[base + m:base + m + blk, :]

    l0, l1, l2 = lse_ref[0], lse_ref[1], lse_ref[2]
    mx = jnp.maximum(jnp.maximum(l0, l1), l2)
    w0, w1, w2 = jnp.exp(l0 - mx), jnp.exp(l1 - mx), jnp.exp(l2 - mx)
    o_ref[...] = (w0 * acc_ref[0] + w1 * acc_ref[1] + w2 * acc_ref[2]) / (w0 + w1 + w2)


def _attention(qkv, n_kv):
    b, _, l, e = qkv.shape
    n_pat = len(ATT_PATTERNS)
    unit = ATT_UNIT
    tile = (None, None, unit, e)
    head = lambda off: pl.BlockSpec(tile, lambda bi, h, u, off=off: (bi, off + h, u, 0))
    staged = []
    for _, dil in ATT_PATTERNS:
        rows = dil * (unit // dil + ATT_BLOCK)
        staged += [pltpu.VMEM((rows, e), BF16), pltpu.VMEM((rows, e), BF16)]
    return pl.pallas_call(
        _attn_kernel,
        grid=(b, n_kv, l // unit),
        in_specs=[head(p * n_kv) for p in range(n_pat)] + [head(n_pat * n_kv),
                                                           head((n_pat + 1) * n_kv)],
        out_specs=pl.BlockSpec((None, unit, e), lambda bi, h, u: (bi, u, h)),
        out_shape=jax.ShapeDtypeStruct((b, l, n_kv * e), F32),
        scratch_shapes=staged + [pltpu.VMEM((n_pat, unit, e), F32),
                                 pltpu.VMEM((n_pat, unit, e), F32),
                                 pltpu.VMEM((unit, e), F32),
                                 pltpu.VMEM((unit, e), F32)],
        compiler_params=_cparams("parallel", "parallel", "arbitrary"),
    )(qkv, qkv, qkv, qkv, qkv)


def kernel(x, c, ada_w, ada_b, mix_pre_g, mix_post_g, ffn_pre_g, ffn_post_g, ffn_w_in, ffn_w_out, hyb_w_in, ssd_conv_w, ssd_conv_b, ssd_dt_bias, ssd_a_log, ssd_d, ssd_norm_g, s5_lambda_re, s5_lambda_im, s5_log_dt, s5_b_re, s5_b_im, s5_c_re, s5_c_im, s5_d, s5_glu_w, s5_glu_b, hyb_w_out, attn_w_qkv, attn_w_o):
    bsz, seq, d = x.shape
    depth = ada_w.shape[0]
    assert seq % ATT_UNIT == 0 and d == SSD_N_HEADS * SSD_HEAD_DIM
    d_inner = d
    conv_dim = ssd_conv_w.shape[-1]
    n_kv = d // ATT_HEAD_DIM

    mod = _ada(c, ada_w, ada_b)
    vec = lambda v: v.reshape(1, d)

    for i in range(depth):
        sh_m, sc_m, gt_m, sh_f, sc_f, gt_f = (
            mod[i, :, k * d:(k + 1) * d].reshape(bsz, 1, d) for k in range(6))
        j = i // 2
        if i % 2 == 0:
            w = hyb_w_in[j]
            o_dt = d_inner + conv_dim
            w_dt = jnp.zeros((d, LANES), F32).at[:, :SSD_N_HEADS].set(w[:, o_dt:o_dt + SSD_N_HEADS])
            w_cat = jnp.concatenate([w[:, :o_dt], w[:, o_dt + SSD_N_HEADS:], w_dt], axis=1).astype(BF16)
            z, xbc, u, dt = _inproj(x, vec(mix_pre_g[i]), sc_m, sh_m, w_cat,
                                    (d_inner, conv_dim, d, LANES), (False, False, True, False))
            y_ssd = _ssd(z, xbc, dt, ssd_conv_w[j], ssd_conv_b[j], ssd_dt_bias[j], ssd_a_log[j],
                         ssd_d[j], ssd_norm_g[j])
            y_s5 = _s5(u, s5_lambda_re[j], s5_lambda_im[j], s5_log_dt[j], s5_b_re[j], s5_b_im[j],
                       s5_c_re[j], s5_c_im[j], s5_d[j], s5_glu_w[j], s5_glu_b[j])
            w_out = hyb_w_out[j].astype(BF16)
            acts, ws = [y_ssd, y_s5], [w_out[:d_inner], w_out[d_inner:]]
        else:
            qkv = _qkvproj(x, vec(mix_pre_g[i]), sc_m, sh_m, attn_w_qkv[j].astype(BF16))
            acts, ws = [_attention(qkv, n_kv)], [attn_w_o[j].astype(BF16)]
        x = _mixout_ffn(acts, ws, x, vec(mix_post_g[i]), gt_m, vec(ffn_pre_g[i]), sc_f, sh_f,
                        ffn_w_in[i].astype(BF16), ffn_w_out[i].astype(BF16),
                        vec(ffn_post_g[i]), gt_f)
    return x
```

```python
import functools
import math

import jax
import jax.numpy as jnp
from jax import lax
from jax.experimental import pallas as pl
from jax.experimental.pallas import tpu as pltpu

F32 = jnp.float32
BF16 = jnp.bfloat16
EPS = 1e-6
HIGHEST = lax.Precision.HIGHEST

LANES = 128
SUBLANES = 8
VMEM_LIMIT = 56 * 1024 * 1024

SSD_HEAD_DIM = 64
SSD_N_HEADS = 16
SSD_N_GROUPS = 2
SSD_D_STATE = 128
SSD_CONV = 4
SSD_CHUNK = 128
S5_GROUP = 16
S5_STATE = 64
ATT_HEAD_DIM = 128
ATT_PATTERNS = ((128, 1), (512, 4), (2048, 16))
ATT_BLOCK = 128
ATT_UNIT = ATT_BLOCK * 16
ATT_QROWS = 128


def _cparams(*sem):
    return pltpu.CompilerParams(dimension_semantics=sem, vmem_limit_bytes=VMEM_LIMIT)


def _resident(shape):
    nd = len(shape)
    return pl.BlockSpec(shape, lambda *_: (0,) * nd, pipeline_mode=pl.Buffered(1))


def _silu(v):
    return v * jax.nn.sigmoid(v)


def _normmod(x, g, sc, sh):
    ms = jnp.mean(x * x, axis=-1, keepdims=True)
    return (x * lax.rsqrt(ms + EPS) * g) * (1.0 + sc) + sh


def _post_residual(x, y, g, gt):
    ms = jnp.mean(y * y, axis=-1, keepdims=True)
    return x + gt * (y * lax.rsqrt(ms + EPS) * g)


def _ada_kernel(c_ref, w_ref, b_ref, o_ref):
    cond = _silu(c_ref[...])
    o_ref[...] = jnp.dot(cond, w_ref[...], precision=HIGHEST,
                         preferred_element_type=F32) + b_ref[...]


def _ada(c, ada_w, ada_b):
    depth, d, n = ada_w.shape
    b = c.shape[0]
    rows = -(-b // SUBLANES) * SUBLANES
    c_pad = jnp.zeros((rows, d), F32).at[:b].set(c)
    tn = 1536
    out = pl.pallas_call(
        _ada_kernel,
        grid=(depth, n // tn),
        in_specs=[pl.BlockSpec((rows, d), lambda i, j: (0, 0)),
                  pl.BlockSpec((None, d, tn), lambda i, j: (i, 0, j)),
                  pl.BlockSpec((None, 1, tn), lambda i, j: (i, 0, j))],
        out_specs=pl.BlockSpec((None, rows, tn), lambda i, j: (i, 0, j)),
        out_shape=jax.ShapeDtypeStruct((depth, rows, n), F32),
        compiler_params=_cparams("arbitrary", "arbitrary"),
    )(c_pad, ada_w, ada_b.reshape(depth, 1, n))
    return out[:, :b]


def _inproj_kernel(x_ref, g_ref, sc_ref, sh_ref, w_ref, *o_refs, splits, slabbed):
    h = _normmod(x_ref[...], g_ref[...], sc_ref[...], sh_ref[...]).astype(BF16)
    off = 0
    for o_ref, n, slab in zip(o_refs, splits, slabbed):
        r = jnp.dot(h, w_ref[:, off:off + n], preferred_element_type=F32)
        if slab:
            for k in range(n // LANES):
                o_ref[k] = r[:, k * LANES:(k + 1) * LANES]
        else:
            o_ref[...] = r
        off += n


def _inproj(x, g, sc, sh, w, splits, slabbed, tm=512):
    b, l, d = x.shape
    row = lambda bi, i: (bi, i, 0)
    per_b = lambda bi, i: (bi, 0, 0)
    slab_row = lambda bi, i: (bi, 0, i, 0)
    return pl.pallas_call(
        functools.partial(_inproj_kernel, splits=splits, slabbed=slabbed),
        grid=(b, l // tm),
        in_specs=[pl.BlockSpec((None, tm, d), row),
                  _resident((1, d)),
                  pl.BlockSpec((None, 1, d), per_b),
                  pl.BlockSpec((None, 1, d), per_b),
                  _resident(w.shape)],
        out_specs=[pl.BlockSpec((None, n // LANES, tm, LANES), slab_row) if s
                   else pl.BlockSpec((None, tm, n), row) for n, s in zip(splits, slabbed)],
        out_shape=[jax.ShapeDtypeStruct((b, n // LANES, l, LANES) if s else (b, l, n), F32)
                   for n, s in zip(splits, slabbed)],
        compiler_params=_cparams("parallel", "parallel"),
    )(x, g, sc, sh, w)


def _qkvproj_kernel(x_ref, g_ref, sc_ref, sh_ref, w_ref, o_ref, *, n_heads):
    h = _normmod(x_ref[...], g_ref[...], sc_ref[...], sh_ref[...]).astype(BF16)
    group = 8
    for h0 in range(0, n_heads, group):
        r = jnp.dot(h, w_ref[:, h0 * ATT_HEAD_DIM:(h0 + group) * ATT_HEAD_DIM],
                    preferred_element_type=F32)
        for k in range(group):
            o_ref[h0 + k] = r[:, k * ATT_HEAD_DIM:(k + 1) * ATT_HEAD_DIM]


def _qkvproj(x, g, sc, sh, w, tm=512):
    b, l, d = x.shape
    n_heads = w.shape[1] // ATT_HEAD_DIM
    per_b = lambda bi, i: (bi, 0, 0)
    return pl.pallas_call(
        functools.partial(_qkvproj_kernel, n_heads=n_heads),
        grid=(b, l // tm),
        in_specs=[pl.BlockSpec((None, tm, d), lambda bi, i: (bi, i, 0)),
                  _resident((1, d)),
                  pl.BlockSpec((None, 1, d), per_b),
                  pl.BlockSpec((None, 1, d), per_b),
                  _resident(w.shape)],
        out_specs=pl.BlockSpec((None, n_heads, tm, ATT_HEAD_DIM), lambda bi, i: (bi, 0, i, 0)),
        out_shape=jax.ShapeDtypeStruct((b, n_heads, l, ATT_HEAD_DIM), F32),
        compiler_params=_cparams("parallel", "parallel"),
    )(x, g, sc, sh, w)


def _mixout_ffn_kernel(*refs, n_in, hidden, chunk):
    acts = refs[:n_in]
    ws = refs[n_in:2 * n_in]
    (x_ref, gmix_ref, gtm_ref, gpre_ref, sc_ref, sh_ref, win_ref, wout_ref, gpost_ref, gtf_ref,
     o_ref, act_ref) = refs[2 * n_in:]
    y = None
    for a_ref, w_ref in zip(acts, ws):
        if len(a_ref.shape) == 3:
            a = jnp.concatenate([a_ref[k] for k in range(a_ref.shape[0])], axis=1)
        else:
            a = a_ref[...]
        t = jnp.dot(a.astype(BF16), w_ref[...], preferred_element_type=F32)
        y = t if y is None else y + t
    x = _post_residual(x_ref[...], y, gmix_ref[...], gtm_ref[...])

    h = _normmod(x, gpre_ref[...], sc_ref[...], sh_ref[...]).astype(BF16)
    for c0 in range(0, hidden, chunk):
        gate = jnp.dot(h, win_ref[:, c0:c0 + chunk], preferred_element_type=F32)
        up = jnp.dot(h, win_ref[:, hidden + c0:hidden + c0 + chunk], preferred_element_type=F32)
        act_ref[:, c0:c0 + chunk] = (_silu(gate) * up).astype(BF16)
    y = jnp.dot(act_ref[...], wout_ref[...], preferred_element_type=F32)
    o_ref[...] = _post_residual(x, y, gpost_ref[...], gtf_ref[...])


def _mixout_ffn(acts, ws, x, gmix, gtm, gpre, sc, sh, win, wout, gpost, gtf, tm=512, chunk=256):
    b, l, d = x.shape
    hidden = wout.shape[0]
    row = lambda bi, i: (bi, i, 0)
    per_b = lambda bi, i: (bi, 0, 0)
    n_in = len(acts)
    act_spec = lambda a: (pl.BlockSpec((None, tm, a.shape[-1]), row) if a.ndim == 3 else
                          pl.BlockSpec((None, a.shape[1], tm, LANES), lambda bi, i: (bi, 0, i, 0)))
    vec = _resident((1, d))
    mod = pl.BlockSpec((None, 1, d), per_b)
    return pl.pallas_call(
        functools.partial(_mixout_ffn_kernel, n_in=n_in, hidden=hidden, chunk=chunk),
        grid=(b, l // tm),
        in_specs=([act_spec(a) for a in acts] + [_resident(w.shape) for w in ws]
                  + [pl.BlockSpec((None, tm, d), row), vec, mod, vec, mod, mod,
                     _resident(win.shape), _resident(wout.shape), vec, mod]),
        out_specs=pl.BlockSpec((None, tm, d), row),
        out_shape=jax.ShapeDtypeStruct((b, l, d), F32),
        scratch_shapes=[pltpu.VMEM((tm, hidden), BF16)],
        compiler_params=_cparams("parallel", "parallel"),
    )(*acts, *ws, x, gmix, gtm, gpre, sc, sh, win, wout, gpost, gtf)


def _split2(v):
    hi = v.astype(BF16)
    return hi, (v - hi.astype(F32)).astype(BF16)


def _split3(v):
    hi = v.astype(BF16)
    r = v - hi.astype(F32)
    mid = r.astype(BF16)
    return hi, mid, (r - mid.astype(F32)).astype(BF16)


def _expand(parts, e):
    out = None
    for p in parts:
        t = jnp.dot(p, e, preferred_element_type=F32)
        out = t if out is None else out + t
    return out


def _ssd_kernel(z_ref, xbc_ref, dt_ref, cw_ref, cb_ref, dtb_ref, alog_ref, dsk_ref, ng_ref,
                e64_ref, e128_ref, o_ref, ext_ref, h_ref, xc_ref, *, ts):
    t = SSD_CHUNK
    d_inner = SSD_N_HEADS * SSD_HEAD_DIM
    gw = d_inner // SSD_N_GROUPS
    hpg = SSD_N_HEADS // SSD_N_GROUPS
    pad = SUBLANES

    @pl.when(pl.program_id(1) == 0)
    def _():
        ext_ref[0:pad, :] = jnp.zeros((pad, ext_ref.shape[1]), F32)
        h_ref[...] = jnp.zeros(h_ref.shape, F32)

    ext_ref[pad:pad + ts, :] = xbc_ref[...]

    li = lax.broadcasted_iota(jnp.int32, (t, t), 0)
    si = lax.broadcasted_iota(jnp.int32, (t, t), 1)
    causal = li >= si
    tril = causal.astype(F32)
    lane = lax.broadcasted_iota(jnp.int32, (t, LANES), 1)
    low_half = lane < SSD_HEAD_DIM
    a_neg = -jnp.exp(alog_ref[...])
    e64 = e64_ref[...]
    e128 = e128_ref[...]

    for c in range(ts // t):
        r0 = c * t
        rows = 64
        for c0 in range(0, ext_ref.shape[1], LANES):
            cols = slice(c0, c0 + LANES)
            for rr in range(r0, r0 + t, rows):
                base = pad + rr - (SSD_CONV - 1)
                acc = cb_ref[:, cols] + cw_ref[0:1, cols] * ext_ref[base:base + rows, cols]
                for k in range(1, SSD_CONV):
                    acc = acc + cw_ref[k:k + 1, cols] * ext_ref[base + k:base + k + rows, cols]
                xc_ref[rr - r0:rr - r0 + rows, cols] = _silu(acc)
        xc = xc_ref[...]
        xs = xc[:, :d_inner]
        dt = jax.nn.softplus(dt_ref[r0:r0 + t, :] + dtb_ref[...])
        ac = jnp.dot(tril, dt * a_neg, precision=HIGHEST, preferred_element_type=F32)
        ac_t = ac.T
        ac_last = ac[t - 1:t, :]
        dt_e = _expand(_split2(dt), e64)
        sdec_e = _expand(_split2(dt * jnp.exp(ac_last - ac)), e64)
        eac_e = _expand(_split2(jnp.exp(ac)), e64)
        cdec_e = _expand(_split2(jnp.broadcast_to(jnp.exp(ac_last), (SUBLANES, LANES))), e64)[0:1, :]
        ac_e = _expand(_split3(ac), e128)
        x_dt = (xs * dt_e).astype(BF16)
        x_st = (xs * sdec_e).astype(BF16)
        ys = []
        for g in range(SSD_N_GROUPS):
            bm = xc[:, d_inner + g * SSD_D_STATE:d_inner + (g + 1) * SSD_D_STATE]
            cm = xc[:, d_inner + (SSD_N_GROUPS + g) * SSD_D_STATE:
                    d_inner + (SSD_N_GROUPS + g + 1) * SSD_D_STATE].astype(BF16)
            cb = lax.dot_general(cm, bm.astype(BF16), (((1,), (1,)), ((), ())),
                                 preferred_element_type=F32)
            cb = jnp.where(causal, cb, 0.0)
            gs = slice(g * gw, (g + 1) * gw)
            h_prev = h_ref[:, gs]
            y_off = jnp.dot(cm, h_prev.astype(BF16), preferred_element_type=F32) * eac_e[:, gs]
            s_new = jnp.dot(bm.T.astype(BF16), x_st[:, gs], preferred_element_type=F32)
            h_ref[:, gs] = h_prev * cdec_e[:, gs] + s_new
            pairs = []
            for q in range(hpg // 2):
                ps = []
                for k in range(2):
                    hh = g * hpg + 2 * q + k
                    diff = ac_e[:, hh * LANES:(hh + 1) * LANES] - ac_t[hh:hh + 1, :]
                    ps.append((cb * jnp.exp(jnp.minimum(diff, 0.0))).astype(BF16))
                xp = x_dt[:, g * gw + q * LANES:g * gw + (q + 1) * LANES]
                zero = jnp.zeros_like(xp)
                lhs = jnp.concatenate(ps, axis=1)
                rhs = jnp.concatenate([jnp.where(low_half, xp, zero),
                                       jnp.where(low_half, zero, xp)], axis=0)
                pairs.append(jnp.dot(lhs, rhs, preferred_element_type=F32))
            ys.append(jnp.concatenate(pairs, axis=1) + y_off)
        y = jnp.concatenate(ys, axis=1) + dsk_ref[...] * xs
        y = y * _silu(z_ref[r0:r0 + t, :])
        outs = []
        for g in range(SSD_N_GROUPS):
            yg = y[:, g * gw:(g + 1) * gw]
            outs.append(yg * lax.rsqrt(jnp.mean(yg * yg, axis=-1, keepdims=True) + EPS))
        o_ref[r0:r0 + t, :] = jnp.concatenate(outs, axis=1) * ng_ref[...]

    ext_ref[0:pad, :] = ext_ref[ts:ts + pad, :]


def _ssd(z, xbc, dt, conv_w, conv_b, dt_bias, a_log, d_skip, norm_g, ts=512):
    b, l, d_inner = z.shape
    cdim = xbc.shape[-1]
    nh = SSD_N_HEADS
    pad1 = lambda v: jnp.zeros((1, LANES), F32).at[0, :nh].set(v)
    head = jnp.arange(LANES)[:, None]
    e64 = (head == (jnp.arange(d_inner) // SSD_HEAD_DIM)[None, :]) & (head < nh)
    e128 = (head == (jnp.arange(nh * LANES) // LANES)[None, :]) & (head < nh)
    row = lambda bi, i: (bi, i, 0)
    return pl.pallas_call(
        functools.partial(_ssd_kernel, ts=ts),
        grid=(b, l // ts),
        in_specs=[pl.BlockSpec((None, ts, d_inner), row),
                  pl.BlockSpec((None, ts, cdim), row),
                  pl.BlockSpec((None, ts, LANES), row),
                  _resident(conv_w.shape),
                  _resident((1, cdim)),
                  _resident((1, LANES)),
                  _resident((1, LANES)),
                  _resident((1, d_inner)),
                  _resident((1, d_inner)),
                  _resident((LANES, d_inner)),
                  _resident((LANES, nh * LANES))],
        out_specs=pl.BlockSpec((None, ts, d_inner), row),
        out_shape=jax.ShapeDtypeStruct((b, l, d_inner), F32),
        scratch_shapes=[pltpu.VMEM((SUBLANES + ts, cdim), F32),
                        pltpu.VMEM((SSD_D_STATE, d_inner), F32),
                        pltpu.VMEM((SSD_CHUNK, cdim), F32)],
        compiler_params=_cparams("parallel", "arbitrary"),
    )(z, xbc, dt, conv_w, conv_b.reshape(1, cdim), pad1(dt_bias), pad1(a_log),
      jnp.repeat(d_skip, SSD_HEAD_DIM).reshape(1, d_inner), norm_g.reshape(1, d_inner),
      e64.astype(BF16), e128.astype(BF16))


def _s5_kernel(u_ref, bmat_ref, cmat_ref, are_ref, aim_ref, asre_ref, asim_ref, dsk_ref,
               gluw_ref, glub_ref, o_ref, pad_ref, up_ref, hre_ref, him_ref,
               cre_ref, cim_ref, y_ref, *, tile):
    seg = tile // SUBLANES
    segp = seg + SUBLANES
    n_slab = u_ref.shape[0]
    sw = hre_ref.shape[-1] // n_slab

    @pl.when(pl.program_id(1) == 0)
    def _():
        cre_ref[...] = jnp.zeros(cre_ref.shape, F32)
        cim_ref[...] = jnp.zeros(cim_ref.shape, F32)

    for j in range(n_slab):
        for k in range(SUBLANES):
            pad_ref[j, k * segp:k * segp + seg, :] = u_ref[j, k * seg:(k + 1) * seg, :]
    for i in range(seg):
        for j in range(n_slab):
            up_ref[i * SUBLANES:(i + 1) * SUBLANES, j * LANES:(j + 1) * LANES] = (
                pad_ref[j, pl.ds(i, SUBLANES, stride=segp), :])

    def in_map(j):
        bu = jnp.dot(up_ref[:, j * LANES:(j + 1) * LANES].astype(BF16), bmat_ref[j],
                     preferred_element_type=F32)
        hre_ref[:, j * sw:(j + 1) * sw] = bu[:, :sw]
        him_ref[:, j * sw:(j + 1) * sw] = bu[:, sw:]

    def scan(j, s_re, s_im, store):
        cols = slice(j * sw, (j + 1) * sw)
        a_re = jnp.broadcast_to(are_ref[:, cols], (SUBLANES, sw))
        a_im = jnp.broadcast_to(aim_ref[:, cols], (SUBLANES, sw))
        for i in range(seg):
            rows = slice(i * SUBLANES, (i + 1) * SUBLANES)
            s_re, s_im = (a_re * s_re - a_im * s_im + hre_ref[rows, cols],
                          a_re * s_im + a_im * s_re + him_ref[rows, cols])
            if store:
                hre_ref[rows, cols] = s_re
                him_ref[rows, cols] = s_im
        return s_re, s_im

    def out_map(j):
        cols = slice(j * sw, (j + 1) * sw)
        ch = slice(j * LANES, (j + 1) * LANES)
        y = (jnp.dot(hre_ref[:, cols].astype(BF16), cmat_ref[j, :sw, :], preferred_element_type=F32)
             + jnp.dot(him_ref[:, cols].astype(BF16), cmat_ref[j, sw:, :], preferred_element_type=F32))
        y = y + dsk_ref[:, ch] * up_ref[:, ch]
        y_ref[:, ch] = 0.5 * y * (1.0 + lax.erf(y * (2.0 ** -0.5)))

    zeros = jnp.zeros((SUBLANES, sw), F32)
    in_map(0)
    for j in range(n_slab):
        if j + 1 < n_slab:
            in_map(j + 1)
        cols = slice(j * sw, (j + 1) * sw)
        e_re, e_im = scan(j, zeros, zeros, store=False)
        s_re, s_im = cre_ref[:, cols], cim_ref[:, cols]
        as_re, as_im = asre_ref[:, cols], asim_ref[:, cols]
        in_re, in_im = [], []
        for k in range(SUBLANES):
            in_re.append(s_re)
            in_im.append(s_im)
            s_re, s_im = (as_re * s_re - as_im * s_im + e_re[k:k + 1, :],
                          as_re * s_im + as_im * s_re + e_im[k:k + 1, :])
        cre_ref[:, cols] = s_re
        cim_ref[:, cols] = s_im
        scan(j, jnp.concatenate(in_re, axis=0), jnp.concatenate(in_im, axis=0), store=True)
        out_map(j)

    y = y_ref[...]
    gate = jnp.dot(y.astype(BF16), gluw_ref[...], preferred_element_type=F32) + glub_ref[...]
    y_ref[...] = y * jax.nn.sigmoid(gate)

    for i in range(seg):
        for j in range(n_slab):
            pad_ref[j, pl.ds(i, SUBLANES, stride=segp), :] = (
                y_ref[i * SUBLANES:(i + 1) * SUBLANES, j * LANES:(j + 1) * LANES])
    for j in range(n_slab):
        for k in range(SUBLANES):
            o_ref[j, k * seg:(k + 1) * seg, :] = pad_ref[j, k * segp:k * segp + seg, :]


def _s5_params(lam_re, lam_im, log_dt, b_re, b_im, c_re, c_im, seg):
    ng, ns = lam_re.shape
    gps = LANES // S5_GROUP
    n_slab = ng // gps
    dt = jnp.exp(log_dt)[:, None]
    mag = jnp.exp(lam_re * dt)
    a_re, a_im = mag * jnp.cos(lam_im * dt), mag * jnp.sin(lam_im * dt)
    den = lam_re * lam_re + lam_im * lam_im
    q_re = ((a_re - 1.0) * lam_re + a_im * lam_im) / den
    q_im = (a_im * lam_re - (a_re - 1.0) * lam_im) / den
    bb_re = q_re[..., None] * b_re - q_im[..., None] * b_im
    bb_im = q_re[..., None] * b_im + q_im[..., None] * b_re
    eye = jnp.eye(gps, dtype=F32)

    def in_map(bb):
        t = bb.reshape(n_slab, gps, ns, S5_GROUP)
        return jnp.einsum('jgpc,gh->jgchp', t, eye).reshape(n_slab, LANES, gps * ns)

    def out_map(cc):
        t = cc.reshape(n_slab, gps, S5_GROUP, ns)
        return jnp.einsum('jgcp,gh->jgphc', t, eye).reshape(n_slab, gps * ns, LANES)

    bmat = jnp.concatenate([in_map(bb_re), in_map(bb_im)], axis=2).astype(BF16)
    cmat = jnp.concatenate([out_map(c_re), -out_map(c_im)], axis=1).astype(BF16)
    as_re, as_im = a_re, a_im
    for _ in range(int(math.log2(seg))):
        as_re, as_im = as_re * as_re - as_im * as_im, 2.0 * as_re * as_im
    flat = lambda v: v.reshape(1, ng * ns)
    return bmat, cmat, flat(a_re), flat(a_im), flat(as_re), flat(as_im)


def _s5(u, lam_re, lam_im, log_dt, b_re, b_im, c_re, c_im, d_skip, glu_w, glu_b, tile=256):
    b, n_slab, l, _ = u.shape
    width = n_slab * LANES
    seg = tile // SUBLANES
    assert seg & (seg - 1) == 0
    bmat, cmat, a_re, a_im, as_re, as_im = _s5_params(lam_re, lam_im, log_dt, b_re, b_im,
                                                      c_re, c_im, seg)
    n_state = a_re.shape[-1]
    row = lambda bi, i: (bi, 0, i, 0)
    vec = _resident((1, n_state))
    return pl.pallas_call(
        functools.partial(_s5_kernel, tile=tile),
        grid=(b, l // tile),
        in_specs=[pl.BlockSpec((None, n_slab, tile, LANES), row),
                  _resident(bmat.shape), _resident(cmat.shape),
                  vec, vec, vec, vec,
                  _resident((1, width)),
                  _resident(glu_w.shape),
                  _resident((1, width))],
        out_specs=pl.BlockSpec((None, n_slab, tile, LANES), row),
        out_shape=jax.ShapeDtypeStruct((b, n_slab, l, LANES), F32),
        scratch_shapes=[pltpu.VMEM((n_slab, tile + SUBLANES * SUBLANES, LANES), F32),
                        pltpu.VMEM((tile, width), F32),
                        pltpu.VMEM((tile, n_state), F32),
                        pltpu.VMEM((tile, n_state), F32),
                        pltpu.VMEM((1, n_state), F32),
                        pltpu.VMEM((1, n_state), F32),
                        pltpu.VMEM((tile, width), F32)],
        compiler_params=_cparams("parallel", "arbitrary"),
    )(u, bmat, cmat, a_re, a_im, as_re, as_im, d_skip.reshape(1, width),
      glu_w.astype(BF16), glu_b.reshape(1, width))


def _attn_kernel(q0_ref, q1_ref, q2_ref, k_ref, v_ref, o_ref,
                 kb0, vb0, kb1, vb1, kb2, vb2, acc_ref, lse_ref, tk_ref, tv_ref, bias_ref):
    blk = ATT_BLOCK
    scale = ATT_HEAD_DIM ** -0.5
    first = pl.program_id(2) == 0
    qi = lax.broadcasted_iota(jnp.int32, (blk, 2 * blk), 0)
    cj = lax.broadcasted_iota(jnp.int32, (blk, 2 * blk), 1)
    band = (cj >= qi) & (cj <= qi + blk)
    neg = jnp.float32(-jnp.inf)
    bias_ref[0] = jnp.where(band, 0.0, neg)
    bias_ref[1] = jnp.where(band & (cj >= jnp.where(first, blk, 0)), 0.0, neg)
    dn = (((1,), (1,)), ((), ()))
    bufs = ((kb0, vb0), (kb1, vb1), (kb2, vb2))

    @pl.when(first)
    def _():
        for kb, vb in bufs:
            kb[...] = jnp.zeros(kb.shape, BF16)
            vb[...] = jnp.zeros(vb.shape, BF16)

    prev_dil = 1
    for (kb, vb), (window, dil) in zip(bufs, ATT_PATTERNS):
        m = ATT_UNIT // dil
        step = dil // prev_dil
        for r in range(dil):
            if dil == 1:
                src = pl.ds(0, m)
            else:
                src = pl.ds((r % prev_dil) * (ATT_UNIT // prev_dil) + r // prev_dil, m, stride=step)
            dst = slice(r * (m + blk) + blk, (r + 1) * (m + blk))
            for src_ref, dst_ref, tmp_ref in ((k_ref, kb, tk_ref), (v_ref, vb, tv_ref)):
                rows = (src_ref if prev_dil == 1 else tmp_ref)[src, :]
                dst_ref[dst, :] = rows.astype(BF16)
                if dil == ATT_PATTERNS[1][1]:
                    tmp_ref[r * m:(r + 1) * m, :] = rows
        prev_dil = dil

    for p, (q_ref, (kb, vb), (window, dil)) in enumerate(
            zip((q0_ref, q1_ref, q2_ref), bufs, ATT_PATTERNS)):
        assert window // dil == blk
        m = ATT_UNIT // dil
        sub = ATT_QROWS
        for r in range(dil):
            for bi in range(m // blk):
                w0 = r * (m + blk) + blk * bi
                win = slice(w0, w0 + 2 * blk)
                for q0 in range(0, blk, sub):
                    start = r + dil * (blk * bi + q0)
                    rows = pl.ds(start, sub, stride=dil) if dil > 1 else pl.ds(start, sub)
                    q = (q_ref[rows, :] * scale).astype(BF16)
                    s = lax.dot_general(q, kb[win, :], dn, preferred_element_type=F32)
                    s = s + bias_ref[1 if bi == 0 else 0, q0:q0 + sub, :]
                    mx = jnp.max(s, axis=-1, keepdims=True)
                    e = jnp.exp(s - mx)
                    den = jnp.sum(e, axis=-1, keepdims=True)
                    o = jnp.dot(e.astype(BF16), vb[win, :], preferred_element_type=F32)
                    acc_ref[p, rows, :] = o / den
                    lse_ref[p, rows, :] = jnp.broadcast_to(mx + jnp.log(den), (sub, ATT_HEAD_DIM))

    for (kb, vb), (window, dil) in zip(bufs, ATT_PATTERNS):
        m = ATT_UNIT // dil
        for r in range(dil):
            base = r * (m + blk)
            kb[base:base + blk, :] = kb[base + m:base + m + blk, :]
            vb[base:base + blk, :] = vb[base + m:base + m + blk, :]

    l0, l1, l2 = lse_ref[0], lse_ref[1], lse_ref[2]
    mx = jnp.maximum(jnp.maximum(l0, l1), l2)
    w0, w1, w2 = jnp.exp(l0 - mx), jnp.exp(l1 - mx), jnp.exp(l2 - mx)
    o_ref[...] = (w0 * acc_ref[0] + w1 * acc_ref[1] + w2 * acc_ref[2]) / (w0 + w1 + w2)


def _attention(qkv, n_kv):
    b, _, l, e = qkv.shape
    n_pat = len(ATT_PATTERNS)
    unit = ATT_UNIT
    tile = (None, None, unit, e)
    head = lambda off: pl.BlockSpec(tile, lambda bi, h, u, off=off: (bi, off + h, u, 0))
    staged = []
    for _, dil in ATT_PATTERNS:
        rows = dil * (unit // dil + ATT_BLOCK)
        staged += [pltpu.VMEM((rows, e), BF16), pltpu.VMEM((rows, e), BF16)]
    return pl.pallas_call(
        _attn_kernel,
        grid=(b, n_kv, l // unit),
        in_specs=[head(p * n_kv) for p in range(n_pat)] + [head(n_pat * n_kv),
                                                           head((n_pat + 1) * n_kv)],
        out_specs=pl.BlockSpec((None, unit, e), lambda bi, h, u: (bi, u, h)),
        out_shape=jax.ShapeDtypeStruct((b, l, n_kv * e), F32),
        scratch_shapes=staged + [pltpu.VMEM((n_pat, unit, e), F32),
                                 pltpu.VMEM((n_pat, unit, e), F32),
                                 pltpu.VMEM((unit, e), F32),
                                 pltpu.VMEM((unit, e), F32),
                                 pltpu.VMEM((2, ATT_BLOCK, 2 * ATT_BLOCK), F32)],
        compiler_params=_cparams("parallel", "parallel", "arbitrary"),
    )(qkv, qkv, qkv, qkv, qkv)


def kernel(x, c, ada_w, ada_b, mix_pre_g, mix_post_g, ffn_pre_g, ffn_post_g, ffn_w_in, ffn_w_out, hyb_w_in, ssd_conv_w, ssd_conv_b, ssd_dt_bias, ssd_a_log, ssd_d, ssd_norm_g, s5_lambda_re, s5_lambda_im, s5_log_dt, s5_b_re, s5_b_im, s5_c_re, s5_c_im, s5_d, s5_glu_w, s5_glu_b, hyb_w_out, attn_w_qkv, attn_w_o):
    bsz, seq, d = x.shape
    depth = ada_w.shape[0]
    assert seq % ATT_UNIT == 0 and d == SSD_N_HEADS * SSD_HEAD_DIM
    d_inner = d
    conv_dim = ssd_conv_w.shape[-1]
    n_kv = d // ATT_HEAD_DIM

    mod = _ada(c, ada_w, ada_b)
    vec = lambda v: v.reshape(1, d)

    for i in range(depth):
        sh_m, sc_m, gt_m, sh_f, sc_f, gt_f = (
            mod[i, :, k * d:(k + 1) * d].reshape(bsz, 1, d) for k in range(6))
        j = i // 2
        if i % 2 == 0:
            w = hyb_w_in[j]
            o_dt = d_inner + conv_dim
            w_dt = jnp.zeros((d, LANES), F32).at[:, :SSD_N_HEADS].set(w[:, o_dt:o_dt + SSD_N_HEADS])
            w_cat = jnp.concatenate([w[:, :o_dt], w[:, o_dt + SSD_N_HEADS:], w_dt], axis=1).astype(BF16)
            z, xbc, u, dt = _inproj(x, vec(mix_pre_g[i]), sc_m, sh_m, w_cat,
                                    (d_inner, conv_dim, d, LANES), (False, False, True, False))
            y_ssd = _ssd(z, xbc, dt, ssd_conv_w[j], ssd_conv_b[j], ssd_dt_bias[j], ssd_a_log[j],
                         ssd_d[j], ssd_norm_g[j])
            y_s5 = _s5(u, s5_lambda_re[j], s5_lambda_im[j], s5_log_dt[j], s5_b_re[j], s5_b_im[j],
                       s5_c_re[j], s5_c_im[j], s5_d[j], s5_glu_w[j], s5_glu_b[j])
            w_out = hyb_w_out[j].astype(BF16)
            acts, ws = [y_ssd, y_s5], [w_out[:d_inner], w_out[d_inner:]]
        else:
            qkv = _qkvproj(x, vec(mix_pre_g[i]), sc_m, sh_m, attn_w_qkv[j].astype(BF16))
            acts, ws = [_attention(qkv, n_kv)], [attn_w_o[j].astype(BF16)]
        x = _mixout_ffn(acts, ws, x, vec(mix_post_g[i]), gt_m, vec(ffn_pre_g[i]), sc_f, sh_f,
                        ffn_w_in[i].astype(BF16), ffn_w_out[i].astype(BF16),
                        vec(ffn_post_g[i]), gt_f)
    return x
```

```python
import functools

import jax
import jax.numpy as jnp
from jax import lax
from jax.experimental import pallas as pl
from jax.experimental.pallas import tpu as pltpu

F32 = jnp.float32
BF16 = jnp.bfloat16
EPS = 1e-6
HIGHEST = lax.Precision.HIGHEST

LANES = 128
SUBLANES = 8
VMEM_LIMIT = 56 * 1024 * 1024

SSD_HEAD_DIM = 64
SSD_N_HEADS = 16
SSD_N_GROUPS = 2
SSD_D_STATE = 128
SSD_CONV = 4
SSD_CHUNK = 128
S5_GROUP = 16
S5_STATE = 64
ATT_HEAD_DIM = 128
ATT_PATTERNS = ((128, 1), (512, 4), (2048, 16))
ATT_BLOCK = 128
ATT_UNIT = ATT_BLOCK * 16
ATT_QROWS = 128


def _cparams(*sem):
    return pltpu.CompilerParams(dimension_semantics=sem, vmem_limit_bytes=VMEM_LIMIT)


def _resident(shape):
    nd = len(shape)
    return pl.BlockSpec(shape, lambda *_: (0,) * nd, pipeline_mode=pl.Buffered(1))


def _silu(v):
    return v * jax.nn.sigmoid(v)


def _normmod(x, g, sc, sh):
    ms = jnp.mean(x * x, axis=-1, keepdims=True)
    return (x * lax.rsqrt(ms + EPS) * g) * (1.0 + sc) + sh


def _post_residual(x, y, g, gt):
    ms = jnp.mean(y * y, axis=-1, keepdims=True)
    return x + gt * (y * lax.rsqrt(ms + EPS) * g)


def _ada_kernel(c_ref, w_ref, b_ref, o_ref):
    cond = _silu(c_ref[...])
    o_ref[...] = jnp.dot(cond, w_ref[...], precision=HIGHEST,
                         preferred_element_type=F32) + b_ref[...]


def _ada(c, ada_w, ada_b):
    depth, d, n = ada_w.shape
    b = c.shape[0]
    rows = -(-b // SUBLANES) * SUBLANES
    c_pad = jnp.zeros((rows, d), F32).at[:b].set(c)
    tn = 1536
    out = pl.pallas_call(
        _ada_kernel,
        grid=(depth, n // tn),
        in_specs=[pl.BlockSpec((rows, d), lambda i, j: (0, 0)),
                  pl.BlockSpec((None, d, tn), lambda i, j: (i, 0, j)),
                  pl.BlockSpec((None, 1, tn), lambda i, j: (i, 0, j))],
        out_specs=pl.BlockSpec((None, rows, tn), lambda i, j: (i, 0, j)),
        out_shape=jax.ShapeDtypeStruct((depth, rows, n), F32),
        compiler_params=_cparams("arbitrary", "arbitrary"),
    )(c_pad, ada_w, ada_b.reshape(depth, 1, n))
    return out[:, :b]


def _inproj_kernel(x_ref, g_ref, sc_ref, sh_ref, w_ref, *o_refs, splits, slabbed):
    h = _normmod(x_ref[...], g_ref[...], sc_ref[...], sh_ref[...]).astype(BF16)
    off = 0
    for o_ref, n, slab in zip(o_refs, splits, slabbed):
        r = jnp.dot(h, w_ref[:, off:off + n], preferred_element_type=F32)
        if slab:
            for k in range(n // LANES):
                o_ref[k] = r[:, k * LANES:(k + 1) * LANES]
        else:
            o_ref[...] = r
        off += n


def _inproj(x, g, sc, sh, w, splits, slabbed, tm=512):
    b, l, d = x.shape
    row = lambda bi, i: (bi, i, 0)
    per_b = lambda bi, i: (bi, 0, 0)
    slab_row = lambda bi, i: (bi, 0, i, 0)
    return pl.pallas_call(
        functools.partial(_inproj_kernel, splits=splits, slabbed=slabbed),
        grid=(b, l // tm),
        in_specs=[pl.BlockSpec((None, tm, d), row),
                  _resident((1, d)),
                  pl.BlockSpec((None, 1, d), per_b),
                  pl.BlockSpec((None, 1, d), per_b),
                  _resident(w.shape)],
        out_specs=[pl.BlockSpec((None, n // LANES, tm, LANES), slab_row) if s
                   else pl.BlockSpec((None, tm, n), row) for n, s in zip(splits, slabbed)],
        out_shape=[jax.ShapeDtypeStruct((b, n // LANES, l, LANES) if s else (b, l, n), F32)
                   for n, s in zip(splits, slabbed)],
        compiler_params=_cparams("parallel", "parallel"),
    )(x, g, sc, sh, w)


def _qkvproj_kernel(x_ref, g_ref, sc_ref, sh_ref, w_ref, o_ref, *, n_heads):
    h = _normmod(x_ref[...], g_ref[...], sc_ref[...], sh_ref[...]).astype(BF16)
    group = 8
    for h0 in range(0, n_heads, group):
        r = jnp.dot(h, w_ref[:, h0 * ATT_HEAD_DIM:(h0 + group) * ATT_HEAD_DIM],
                    preferred_element_type=F32)
        for k in range(group):
            o_ref[h0 + k] = r[:, k * ATT_HEAD_DIM:(k + 1) * ATT_HEAD_DIM]


def _qkvproj(x, g, sc, sh, w, tm=512):
    b, l, d = x.shape
    n_heads = w.shape[1] // ATT_HEAD_DIM
    per_b = lambda bi, i: (bi, 0, 0)
    return pl.pallas_call(
        functools.partial(_qkvproj_kernel, n_heads=n_heads),
        grid=(b, l // tm),
        in_specs=[pl.BlockSpec((None, tm, d), lambda bi, i: (bi, i, 0)),
                  _resident((1, d)),
                  pl.BlockSpec((None, 1, d), per_b),
                  pl.BlockSpec((None, 1, d), per_b),
                  _resident(w.shape)],
        out_specs=pl.BlockSpec((None, n_heads, tm, ATT_HEAD_DIM), lambda bi, i: (bi, 0, i, 0)),
        out_shape=jax.ShapeDtypeStruct((b, n_heads, l, ATT_HEAD_DIM), F32),
        compiler_params=_cparams("parallel", "parallel"),
    )(x, g, sc, sh, w)


def _mixout_ffn_kernel(*refs, n_in, glu, hidden, chunk):
    acts = refs[:n_in]
    ws = refs[n_in:2 * n_in]
    refs = refs[2 * n_in:]
    if glu:
        gluw_ref, glub_ref = refs[:2]
        refs = refs[2:]
    (x_ref, gmix_ref, gtm_ref, gpre_ref, sc_ref, sh_ref, win_ref, wout_ref, gpost_ref, gtf_ref,
     o_ref, act_ref) = refs
    y = None
    for k_in, (a_ref, w_ref) in enumerate(zip(acts, ws)):
        if len(a_ref.shape) == 3:
            a = jnp.concatenate([a_ref[k] for k in range(a_ref.shape[0])], axis=1)
        else:
            a = a_ref[...]
        if glu and k_in == n_in - 1:
            a = 0.5 * a * (1.0 + lax.erf(a * (2.0 ** -0.5)))
            gate = jnp.dot(a.astype(BF16), gluw_ref[...], preferred_element_type=F32) + glub_ref[...]
            a = a * jax.nn.sigmoid(gate)
        t = jnp.dot(a.astype(BF16), w_ref[...], preferred_element_type=F32)
        y = t if y is None else y + t
    x = _post_residual(x_ref[...], y, gmix_ref[...], gtm_ref[...])

    h = _normmod(x, gpre_ref[...], sc_ref[...], sh_ref[...]).astype(BF16)
    for c0 in range(0, hidden, chunk):
        gate = jnp.dot(h, win_ref[:, c0:c0 + chunk], preferred_element_type=F32)
        up = jnp.dot(h, win_ref[:, hidden + c0:hidden + c0 + chunk], preferred_element_type=F32)
        act_ref[:, c0:c0 + chunk] = (_silu(gate) * up).astype(BF16)
    y = jnp.dot(act_ref[...], wout_ref[...], preferred_element_type=F32)
    o_ref[...] = _post_residual(x, y, gpost_ref[...], gtf_ref[...])


def _mixout_ffn(acts, ws, glu, x, gmix, gtm, gpre, sc, sh, win, wout, gpost, gtf, tm=512, chunk=256):
    b, l, d = x.shape
    hidden = wout.shape[0]
    row = lambda bi, i: (bi, i, 0)
    per_b = lambda bi, i: (bi, 0, 0)
    n_in = len(acts)
    act_spec = lambda a: (pl.BlockSpec((None, tm, a.shape[-1]), row) if a.ndim == 3 else
                          pl.BlockSpec((None, a.shape[1], tm, LANES), lambda bi, i: (bi, 0, i, 0)))
    vec = _resident((1, d))
    mod = pl.BlockSpec((None, 1, d), per_b)
    glu_args = [] if glu is None else list(glu)
    return pl.pallas_call(
        functools.partial(_mixout_ffn_kernel, n_in=n_in, glu=glu is not None, hidden=hidden,
                          chunk=chunk),
        grid=(b, l // tm),
        in_specs=([act_spec(a) for a in acts] + [_resident(w.shape) for w in ws]
                  + [_resident(g.shape) for g in glu_args]
                  + [pl.BlockSpec((None, tm, d), row), vec, mod, vec, mod, mod,
                     _resident(win.shape), _resident(wout.shape), vec, mod]),
        out_specs=pl.BlockSpec((None, tm, d), row),
        out_shape=jax.ShapeDtypeStruct((b, l, d), F32),
        scratch_shapes=[pltpu.VMEM((tm, hidden), BF16)],
        compiler_params=_cparams("parallel", "parallel"),
    )(*acts, *ws, *glu_args, x, gmix, gtm, gpre, sc, sh, win, wout, gpost, gtf)


def _split2(v):
    hi = v.astype(BF16)
    return hi, (v - hi.astype(F32)).astype(BF16)


def _split3(v):
    hi = v.astype(BF16)
    r = v - hi.astype(F32)
    mid = r.astype(BF16)
    return hi, mid, (r - mid.astype(F32)).astype(BF16)


def _expand(parts, e):
    out = None
    for p in parts:
        t = jnp.dot(p, e, preferred_element_type=F32)
        out = t if out is None else out + t
    return out


def _ssd_kernel(z_ref, xbc_ref, dt_ref, cw_ref, cb_ref, dtb_ref, alog_ref, dsk_ref, ng_ref,
                e64_ref, e128_ref, o_ref, ext_ref, h_ref, xc_ref, *, ts):
    t = SSD_CHUNK
    d_inner = SSD_N_HEADS * SSD_HEAD_DIM
    gw = d_inner // SSD_N_GROUPS
    hpg = SSD_N_HEADS // SSD_N_GROUPS
    pad = SUBLANES

    @pl.when(pl.program_id(1) == 0)
    def _():
        ext_ref[0:pad, :] = jnp.zeros((pad, ext_ref.shape[1]), F32)
        h_ref[...] = jnp.zeros(h_ref.shape, F32)

    ext_ref[pad:pad + ts, :] = xbc_ref[...]

    li = lax.broadcasted_iota(jnp.int32, (t, t), 0)
    si = lax.broadcasted_iota(jnp.int32, (t, t), 1)
    causal = li >= si
    tril = causal.astype(F32)
    lane = lax.broadcasted_iota(jnp.int32, (t, LANES), 1)
    low_half = lane < SSD_HEAD_DIM
    a_neg = -jnp.exp(alog_ref[...])
    e64 = e64_ref[...]
    e128 = e128_ref[...]

    for c in range(ts // t):
        r0 = c * t
        rows = 64
        for c0 in range(0, ext_ref.shape[1], LANES):
            cols = slice(c0, c0 + LANES)
            for rr in range(r0, r0 + t, rows):
                base = pad + rr - (SSD_CONV - 1)
                acc = cb_ref[:, cols] + cw_ref[0:1, cols] * ext_ref[base:base + rows, cols]
                for k in range(1, SSD_CONV):
                    acc = acc + cw_ref[k:k + 1, cols] * ext_ref[base + k:base + k + rows, cols]
                xc_ref[rr - r0:rr - r0 + rows, cols] = _silu(acc)
        xc = xc_ref[...]
        xs = xc[:, :d_inner]
        dt = jax.nn.softplus(dt_ref[r0:r0 + t, :] + dtb_ref[...])
        ac = jnp.dot(tril, dt * a_neg, precision=HIGHEST, preferred_element_type=F32)
        ac_t = ac.T
        ac_last = ac[t - 1:t, :]
        dt_e = _expand(_split2(dt), e64)
        sdec_e = _expand(_split2(dt * jnp.exp(ac_last - ac)), e64)
        eac_e = _expand(_split2(jnp.exp(ac)), e64)
        cdec_e = _expand(_split2(jnp.broadcast_to(jnp.exp(ac_last), (SUBLANES, LANES))), e64)[0:1, :]
        ac_e = _expand(_split3(ac), e128)
        x_dt = (xs * dt_e).astype(BF16)
        x_st = (xs * sdec_e).astype(BF16)
        ys = []
        for g in range(SSD_N_GROUPS):
            bm = xc[:, d_inner + g * SSD_D_STATE:d_inner + (g + 1) * SSD_D_STATE]
            cm = xc[:, d_inner + (SSD_N_GROUPS + g) * SSD_D_STATE:
                    d_inner + (SSD_N_GROUPS + g + 1) * SSD_D_STATE].astype(BF16)
            cb = lax.dot_general(cm, bm.astype(BF16), (((1,), (1,)), ((), ())),
                                 preferred_element_type=F32)
            cb = jnp.where(causal, cb, 0.0)
            gs = slice(g * gw, (g + 1) * gw)
            h_prev = h_ref[:, gs]
            y_off = jnp.dot(cm, h_prev.astype(BF16), preferred_element_type=F32) * eac_e[:, gs]
            s_new = jnp.dot(bm.T.astype(BF16), x_st[:, gs], preferred_element_type=F32)
            h_ref[:, gs] = h_prev * cdec_e[:, gs] + s_new
            pairs = []
            for q in range(hpg // 2):
                ps = []
                for k in range(2):
                    hh = g * hpg + 2 * q + k
                    diff = ac_e[:, hh * LANES:(hh + 1) * LANES] - ac_t[hh:hh + 1, :]
                    ps.append((cb * jnp.exp(jnp.minimum(diff, 0.0))).astype(BF16))
                xp = x_dt[:, g * gw + q * LANES:g * gw + (q + 1) * LANES]
                zero = jnp.zeros_like(xp)
                lhs = jnp.concatenate(ps, axis=1)
                rhs = jnp.concatenate([jnp.where(low_half, xp, zero),
                                       jnp.where(low_half, zero, xp)], axis=0)
                pairs.append(jnp.dot(lhs, rhs, preferred_element_type=F32))
            ys.append(jnp.concatenate(pairs, axis=1) + y_off)
        y = jnp.concatenate(ys, axis=1) + dsk_ref[...] * xs
        y = y * _silu(z_ref[r0:r0 + t, :])
        outs = []
        for g in range(SSD_N_GROUPS):
            yg = y[:, g * gw:(g + 1) * gw]
            outs.append(yg * lax.rsqrt(jnp.mean(yg * yg, axis=-1, keepdims=True) + EPS))
        o_ref[r0:r0 + t, :] = jnp.concatenate(outs, axis=1) * ng_ref[...]

    ext_ref[0:pad, :] = ext_ref[ts:ts + pad, :]


def _ssd(z, xbc, dt, conv_w, conv_b, dt_bias, a_log, d_skip, norm_g, ts=512):
    b, l, d_inner = z.shape
    cdim = xbc.shape[-1]
    nh = SSD_N_HEADS
    pad1 = lambda v: jnp.zeros((1, LANES), F32).at[0, :nh].set(v)
    head = jnp.arange(LANES)[:, None]
    e64 = (head == (jnp.arange(d_inner) // SSD_HEAD_DIM)[None, :]) & (head < nh)
    e128 = (head == (jnp.arange(nh * LANES) // LANES)[None, :]) & (head < nh)
    row = lambda bi, i: (bi, i, 0)
    return pl.pallas_call(
        functools.partial(_ssd_kernel, ts=ts),
        grid=(b, l // ts),
        in_specs=[pl.BlockSpec((None, ts, d_inner), row),
                  pl.BlockSpec((None, ts, cdim), row),
                  pl.BlockSpec((None, ts, LANES), row),
                  _resident(conv_w.shape),
                  _resident((1, cdim)),
                  _resident((1, LANES)),
                  _resident((1, LANES)),
                  _resident((1, d_inner)),
                  _resident((1, d_inner)),
                  _resident((LANES, d_inner)),
                  _resident((LANES, nh * LANES))],
        out_specs=pl.BlockSpec((None, ts, d_inner), row),
        out_shape=jax.ShapeDtypeStruct((b, l, d_inner), F32),
        scratch_shapes=[pltpu.VMEM((SUBLANES + ts, cdim), F32),
                        pltpu.VMEM((SSD_D_STATE, d_inner), F32),
                        pltpu.VMEM((SSD_CHUNK, cdim), F32)],
        compiler_params=_cparams("parallel", "arbitrary"),
    )(z, xbc, dt, conv_w, conv_b.reshape(1, cdim), pad1(dt_bias), pad1(a_log),
      jnp.repeat(d_skip, SSD_HEAD_DIM).reshape(1, d_inner), norm_g.reshape(1, d_inner),
      e64.astype(BF16), e128.astype(BF16))


def _s5_params(lam_re, lam_im, log_dt, b_re, b_im, c_re, c_im):
    ng, ns = lam_re.shape
    gps = LANES // S5_GROUP
    n_slab = ng // gps
    dt = jnp.exp(log_dt)[:, None]
    mag = jnp.exp(lam_re * dt)
    a_re, a_im = mag * jnp.cos(lam_im * dt), mag * jnp.sin(lam_im * dt)
    den = lam_re * lam_re + lam_im * lam_im
    q_re = ((a_re - 1.0) * lam_re + a_im * lam_im) / den
    q_im = (a_im * lam_re - (a_re - 1.0) * lam_im) / den
    bb_re = q_re[..., None] * b_re - q_im[..., None] * b_im
    bb_im = q_re[..., None] * b_im + q_im[..., None] * b_re
    eye = jnp.eye(gps, dtype=F32)

    def in_map(bb):
        t = bb.reshape(n_slab, gps, ns, S5_GROUP)
        return jnp.einsum('jgpc,gh->jgchp', t, eye).reshape(n_slab, LANES, gps * ns)

    def out_map(cc):
        t = cc.reshape(n_slab, gps, S5_GROUP, ns)
        return jnp.einsum('jgcp,gh->jgphc', t, eye).reshape(n_slab, gps * ns, LANES)

    bmat = jnp.concatenate([in_map(bb_re), in_map(bb_im)], axis=2).astype(BF16)
    cmat = jnp.concatenate([out_map(c_re), -out_map(c_im)], axis=1).astype(BF16)
    flat = lambda v: v.reshape(1, ng * ns)
    return bmat, cmat, flat(a_re), flat(a_im)


def _s5_time_kernel(u_ref, bpair_ref, cpair_ref, are_ref, aim_ref, dsk_ref, o_ref,
                    lhs_ref, hre_ref, him_ref, cre_ref, cim_ref, y_ref):
    nb, n_slab, steps, _ = u_ref.shape
    n_pair = n_slab // 2
    rows = steps * SUBLANES
    sw = hre_ref.shape[-1] // n_pair
    assert 2 * nb == SUBLANES

    @pl.when(pl.program_id(0) == 0)
    def _():
        lhs_ref[...] = jnp.zeros(lhs_ref.shape, F32)
        cre_ref[...] = jnp.zeros(cre_ref.shape, F32)
        cim_ref[...] = jnp.zeros(cim_ref.shape, F32)

    for b in range(nb):
        for j in range(n_slab):
            h, jp = j // n_pair, j % n_pair
            lhs_ref[2 * jp + h, pl.ds(h * nb + b, steps, stride=SUBLANES), :] = u_ref[b, j]

    for jp in range(n_pair):
        lhs = jnp.concatenate([lhs_ref[2 * jp], lhs_ref[2 * jp + 1]], axis=1).astype(BF16)
        bu = jnp.dot(lhs, bpair_ref[jp], preferred_element_type=F32)
        hre_ref[:, jp * sw:(jp + 1) * sw] = bu[:, :sw]
        him_ref[:, jp * sw:(jp + 1) * sw] = bu[:, sw:]

    upper = (lax.broadcasted_iota(jnp.int32, (rows, LANES), 0) & nb) == 0
    for jp in range(n_pair):
        cols = slice(jp * sw, (jp + 1) * sw)
        a_re, a_im = are_ref[jp], aim_ref[jp]
        s_re, s_im = cre_ref[:, cols], cim_ref[:, cols]
        for t in range(steps):
            r8 = slice(t * SUBLANES, (t + 1) * SUBLANES)
            s_re, s_im = (a_re * s_re - a_im * s_im + hre_ref[r8, cols],
                          a_re * s_im + a_im * s_re + him_ref[r8, cols])
            hre_ref[r8, cols] = s_re
            him_ref[r8, cols] = s_im
        cre_ref[:, cols] = s_re
        cim_ref[:, cols] = s_im
        z = (jnp.dot(hre_ref[:, cols].astype(BF16), cpair_ref[jp, :sw, :], preferred_element_type=F32)
             + jnp.dot(him_ref[:, cols].astype(BF16), cpair_ref[jp, sw:, :], preferred_element_type=F32))
        u_rows = lhs_ref[2 * jp] + lhs_ref[2 * jp + 1]
        d_rows = jnp.tile(dsk_ref[jp], (steps, 1))
        y_ref[jp] = jnp.where(upper, z[:, :LANES], z[:, LANES:]) + d_rows * u_rows

    for b in range(nb):
        for j in range(n_slab):
            h, jp = j // n_pair, j % n_pair
            o_ref[b, j] = y_ref[jp, pl.ds(h * nb + b, steps, stride=SUBLANES), :]


def _s5_time(u, lam_re, lam_im, log_dt, b_re, b_im, c_re, c_im, d_skip, steps=64):
    b, n_slab, l, _ = u.shape
    n_pair = n_slab // 2
    bmat, cmat, a_re, a_im = _s5_params(lam_re, lam_im, log_dt, b_re, b_im, c_re, c_im)
    sw = a_re.shape[-1] // n_slab
    bpair = jnp.concatenate([bmat[:n_pair], bmat[n_pair:]], axis=1)
    cpair = jnp.concatenate([cmat[:n_pair], cmat[n_pair:]], axis=2)

    def per_row(v, width):
        v = v.reshape(2, n_pair, 1, width)
        return jnp.broadcast_to(v, (2, n_pair, b, width)).transpose(1, 0, 2, 3).reshape(
            n_pair, 2 * b, width)

    rows = steps * SUBLANES
    blk = pl.BlockSpec((b, n_slab, steps, LANES), lambda i: (0, 0, i, 0))
    return pl.pallas_call(
        _s5_time_kernel,
        grid=(l // steps,),
        in_specs=[blk, _resident(bpair.shape), _resident(cpair.shape),
                  _resident((n_pair, 2 * b, sw)), _resident((n_pair, 2 * b, sw)),
                  _resident((n_pair, 2 * b, LANES))],
        out_specs=blk,
        out_shape=jax.ShapeDtypeStruct(u.shape, F32),
        scratch_shapes=[pltpu.VMEM((n_slab, rows, LANES), F32),
                        pltpu.VMEM((rows, n_pair * sw), F32),
                        pltpu.VMEM((rows, n_pair * sw), F32),
                        pltpu.VMEM((2 * b, n_pair * sw), F32),
                        pltpu.VMEM((2 * b, n_pair * sw), F32),
                        pltpu.VMEM((n_pair, rows, LANES), F32)],
        compiler_params=_cparams("arbitrary"),
    )(u, bpair, cpair, per_row(a_re, sw), per_row(a_im, sw),
      per_row(d_skip.reshape(1, n_slab * LANES), LANES))


def _attn_kernel(q0_ref, q1_ref, q2_ref, k_ref, v_ref, o_ref,
                 kb0, vb0, kb1, vb1, kb2, vb2, acc_ref, lse_ref, tk_ref, tv_ref, bias_ref):
    blk = ATT_BLOCK
    scale = ATT_HEAD_DIM ** -0.5
    first = pl.program_id(2) == 0
    qi = lax.broadcasted_iota(jnp.int32, (blk, 2 * blk), 0)
    cj = lax.broadcasted_iota(jnp.int32, (blk, 2 * blk), 1)
    band = (cj >= qi) & (cj <= qi + blk)
    neg = jnp.float32(-jnp.inf)
    bias_ref[0] = jnp.where(band, 0.0, neg)
    bias_ref[1] = jnp.where(band & (cj >= jnp.where(first, blk, 0)), 0.0, neg)
    dn = (((1,), (1,)), ((), ()))
    bufs = ((kb0, vb0), (kb1, vb1), (kb2, vb2))

    @pl.when(first)
    def _():
        for kb, vb in bufs:
            kb[...] = jnp.zeros(kb.shape, BF16)
            vb[...] = jnp.zeros(vb.shape, BF16)

    prev_dil = 1
    for (kb, vb), (window, dil) in zip(bufs, ATT_PATTERNS):
        m = ATT_UNIT // dil
        step = dil // prev_dil
        for r in range(dil):
            if dil == 1:
                src = pl.ds(0, m)
            else:
                src = pl.ds((r % prev_dil) * (ATT_UNIT // prev_dil) + r // prev_dil, m, stride=step)
            dst = slice(r * (m + blk) + blk, (r + 1) * (m + blk))
            for src_ref, dst_ref, tmp_ref in ((k_ref, kb, tk_ref), (v_ref, vb, tv_ref)):
                rows = (src_ref if prev_dil == 1 else tmp_ref)[src, :]
                dst_ref[dst, :] = rows.astype(BF16)
                if dil == ATT_PATTERNS[1][1]:
                    tmp_ref[r * m:(r + 1) * m, :] = rows
        prev_dil = dil

    for p, (q_ref, (kb, vb), (window, dil)) in enumerate(
            zip((q0_ref, q1_ref, q2_ref), bufs, ATT_PATTERNS)):
        assert window // dil == blk
        m = ATT_UNIT // dil
        sub = ATT_QROWS
        for r in range(dil):
            for bi in range(m // blk):
                w0 = r * (m + blk) + blk * bi
                win = slice(w0, w0 + 2 * blk)
                for q0 in range(0, blk, sub):
                    start = r + dil * (blk * bi + q0)
                    rows = pl.ds(start, sub, stride=dil) if dil > 1 else pl.ds(start, sub)
                    q = (q_ref[rows, :] * scale).astype(BF16)
                    s = lax.dot_general(q, kb[win, :], dn, preferred_element_type=F32)
                    s = s + bias_ref[1 if bi == 0 else 0, q0:q0 + sub, :]
                    mx = jnp.max(s, axis=-1, keepdims=True)
                    e = jnp.exp(s - mx)
                    den = jnp.sum(e, axis=-1, keepdims=True)
                    o = jnp.dot(e.astype(BF16), vb[win, :], preferred_element_type=F32)
                    acc_ref[p, rows, :] = o / den
                    lse_ref[p, rows, :] = jnp.broadcast_to(mx + jnp.log(den), (sub, ATT_HEAD_DIM))

    for (kb, vb), (window, dil) in zip(bufs, ATT_PATTERNS):
        m = ATT_UNIT // dil
        for r in range(dil):
            base = r * (m + blk)
            kb[base:base + blk, :] = kb[base + m:base + m + blk, :]
            vb[base:base + blk, :] = vb[base + m:base + m + blk, :]

    l0, l1, l2 = lse_ref[0], lse_ref[1], lse_ref[2]
    mx = jnp.maximum(jnp.maximum(l0, l1), l2)
    w0, w1, w2 = jnp.exp(l0 - mx), jnp.exp(l1 - mx), jnp.exp(l2 - mx)
    o_ref[...] = (w0 * acc_ref[0] + w1 * acc_ref[1] + w2 * acc_ref[2]) / (w0 + w1 + w2)


def _attention(qkv, n_kv):
    b, _, l, e = qkv.shape
    n_pat = len(ATT_PATTERNS)
    unit = ATT_UNIT
    tile = (None, None, unit, e)
    head = lambda off: pl.BlockSpec(tile, lambda bi, h, u, off=off: (bi, off + h, u, 0))
    staged = []
    for _, dil in ATT_PATTERNS:
        rows = dil * (unit // dil + ATT_BLOCK)
        staged += [pltpu.VMEM((rows, e), BF16), pltpu.VMEM((rows, e), BF16)]
    return pl.pallas_call(
        _attn_kernel,
        grid=(b, n_kv, l // unit),
        in_specs=[head(p * n_kv) for p in range(n_pat)] + [head(n_pat * n_kv),
                                                           head((n_pat + 1) * n_kv)],
        out_specs=pl.BlockSpec((None, unit, e), lambda bi, h, u: (bi, u, h)),
        out_shape=jax.ShapeDtypeStruct((b, l, n_kv * e), F32),
        scratch_shapes=staged + [pltpu.VMEM((n_pat, unit, e), F32),
                                 pltpu.VMEM((n_pat, unit, e), F32),
                                 pltpu.VMEM((unit, e), F32),
                                 pltpu.VMEM((unit, e), F32),
                                 pltpu.VMEM((2, ATT_BLOCK, 2 * ATT_BLOCK), F32)],
        compiler_params=_cparams("parallel", "parallel", "arbitrary"),
    )(qkv, qkv, qkv, qkv, qkv)


def kernel(x, c, ada_w, ada_b, mix_pre_g, mix_post_g, ffn_pre_g, ffn_post_g, ffn_w_in, ffn_w_out, hyb_w_in, ssd_conv_w, ssd_conv_b, ssd_dt_bias, ssd_a_log, ssd_d, ssd_norm_g, s5_lambda_re, s5_lambda_im, s5_log_dt, s5_b_re, s5_b_im, s5_c_re, s5_c_im, s5_d, s5_glu_w, s5_glu_b, hyb_w_out, attn_w_qkv, attn_w_o):
    bsz, seq, d = x.shape
    depth = ada_w.shape[0]
    assert seq % ATT_UNIT == 0 and d == SSD_N_HEADS * SSD_HEAD_DIM
    d_inner = d
    conv_dim = ssd_conv_w.shape[-1]
    n_kv = d // ATT_HEAD_DIM

    mod = _ada(c, ada_w, ada_b)
    vec = lambda v: v.reshape(1, d)

    for i in range(depth):
        sh_m, sc_m, gt_m, sh_f, sc_f, gt_f = (
            mod[i, :, k * d:(k + 1) * d].reshape(bsz, 1, d) for k in range(6))
        j = i // 2
        if i % 2 == 0:
            w = hyb_w_in[j]
            o_dt = d_inner + conv_dim
            w_dt = jnp.zeros((d, LANES), F32).at[:, :SSD_N_HEADS].set(w[:, o_dt:o_dt + SSD_N_HEADS])
            w_cat = jnp.concatenate([w[:, :o_dt], w[:, o_dt + SSD_N_HEADS:], w_dt], axis=1).astype(BF16)
            z, xbc, u, dt = _inproj(x, vec(mix_pre_g[i]), sc_m, sh_m, w_cat,
                                    (d_inner, conv_dim, d, LANES), (False, False, True, False))
            y_ssd = _ssd(z, xbc, dt, ssd_conv_w[j], ssd_conv_b[j], ssd_dt_bias[j], ssd_a_log[j],
                         ssd_d[j], ssd_norm_g[j])
            y_s5 = _s5_time(u, s5_lambda_re[j], s5_lambda_im[j], s5_log_dt[j], s5_b_re[j],
                            s5_b_im[j], s5_c_re[j], s5_c_im[j], s5_d[j])
            w_out = hyb_w_out[j].astype(BF16)
            acts, ws = [y_ssd, y_s5], [w_out[:d_inner], w_out[d_inner:]]
            glu = (s5_glu_w[j].astype(BF16), s5_glu_b[j].reshape(1, d))
        else:
            qkv = _qkvproj(x, vec(mix_pre_g[i]), sc_m, sh_m, attn_w_qkv[j].astype(BF16))
            acts, ws, glu = [_attention(qkv, n_kv)], [attn_w_o[j].astype(BF16)], None
        x = _mixout_ffn(acts, ws, glu, x, vec(mix_post_g[i]), gt_m, vec(ffn_pre_g[i]), sc_f, sh_f,
                        ffn_w_in[i].astype(BF16), ffn_w_out[i].astype(BF16),
                        vec(ffn_post_g[i]), gt_f)
    return x
```
